```python
import jax, jax.numpy as jnp
from jax import lax
import numpy as np

D_MODEL = 1024
BATCH = 8
SEQ = 4096
DEPTH = 2
DEC_BATCH = 16
DEC_SEQ = 4096
PAST_LEN = 128

N_MIXERS = 2
N_FGROUPS = 8
F_GROUP = D_MODEL // N_FGROUPS
HEAD_DIM = 128
N_HEADS = D_MODEL // HEAD_DIM
N_KV_HEADS = 2
KV_GROUP = N_HEADS // N_KV_HEADS
QKV_DIM = (N_HEADS + 2 * N_KV_HEADS) * HEAD_DIM
AXIS_DIM = HEAD_DIM // 2
ROPE_THETA = 10000.0
GRID_W = 64
Q_BLOCK = 128
D_FF = 4 * D_MODEL
EPS = 1e-6

kernel_name = "fourier_gqa_axial_hybrid_encoder"


def rms_norm(x, g):
    x32 = x.astype(jnp.float32)
    y = x32 * lax.rsqrt(jnp.mean(x32 * x32, axis=-1, keepdims=True) + EPS)
    return (y * g.astype(jnp.float32)).astype(x.dtype)


def fourier_mix(h, w_out):
    B, S, _ = h.shape
    hg = h.astype(jnp.float32).reshape(B, S, N_FGROUPS, F_GROUP)
    f = jnp.fft.fft2(hg, axes=(1, 3), norm="ortho").real
    f = f.reshape(B, S, D_MODEL).astype(h.dtype)
    return f @ w_out


def axial_angles(S):
    rows = S // GRID_W
    row = jnp.repeat(jnp.arange(rows, dtype=jnp.float32), GRID_W)
    col = jnp.tile(jnp.arange(GRID_W, dtype=jnp.float32), rows)
    inv = ROPE_THETA ** (-jnp.arange(0, AXIS_DIM, 2, dtype=jnp.float32) / AXIS_DIM)
    return row[:, None] * inv[None, :], col[:, None] * inv[None, :]


def rope_1d(x, ang):
    c = jnp.cos(ang)[:, None, :]
    s = jnp.sin(ang)[:, None, :]
    x1, x2 = jnp.split(x, 2, axis=-1)
    return jnp.concatenate([x1 * c - x2 * s, x2 * c + x1 * s], axis=-1)


def axial_rope(x, ang_row, ang_col):
    x32 = x.astype(jnp.float32)
    xr, xc = jnp.split(x32, 2, axis=-1)
    return jnp.concatenate([rope_1d(xr, ang_row), rope_1d(xc, ang_col)], axis=-1)


def gqa_attention(h, w_qkv, q_gain, k_gain, w_o):
    B, S, _ = h.shape
    qkv = h @ w_qkv
    q = qkv[..., : N_HEADS * HEAD_DIM].reshape(B, S, N_HEADS, HEAD_DIM)
    k = qkv[..., N_HEADS * HEAD_DIM:(N_HEADS + N_KV_HEADS) * HEAD_DIM].reshape(B, S, N_KV_HEADS, HEAD_DIM)
    v = qkv[..., (N_HEADS + N_KV_HEADS) * HEAD_DIM:].reshape(B, S, N_KV_HEADS, HEAD_DIM)
    q = rms_norm(q, q_gain)
    k = rms_norm(k, k_gain)
    ang_row, ang_col = axial_angles(S)
    q = axial_rope(q, ang_row, ang_col) * (HEAD_DIM ** -0.5)
    k = axial_rope(k, ang_row, ang_col)
    n_blk = S // Q_BLOCK
    qb = q.reshape(B, n_blk, Q_BLOCK, N_KV_HEADS, KV_GROUP, HEAD_DIM).transpose(1, 0, 2, 3, 4, 5)

    def one_block(q_blk):
        s = jnp.einsum("bqkgd,bskd->bkgqs", q_blk, k)
        p = jax.nn.softmax(s, axis=-1)
        return jnp.einsum("bkgqs,bskd->bqkgd", p.astype(v.dtype), v)

    o = lax.map(one_block, qb)
    o = o.transpose(1, 0, 2, 3, 4, 5).reshape(B, S, N_HEADS * HEAD_DIM).astype(h.dtype)
    return o @ w_o


def sq_relu_mlp(h, w_up, w_down):
    a = jnp.maximum(h @ w_up, 0)
    return (a * a) @ w_down


def trunk(x, fourier_norm, fourier_w_out, attn_norm, attn_w_qkv, attn_q_norm, attn_k_norm,
          attn_w_o, mlp_norm, mlp_w_up, mlp_w_down, final_norm):
    for i in range(DEPTH):
        j = i // N_MIXERS
        if i % N_MIXERS == 0:
            x = x + fourier_mix(rms_norm(x, fourier_norm[j]), fourier_w_out[j])
        else:
            x = x + gqa_attention(rms_norm(x, attn_norm[j]), attn_w_qkv[j], attn_q_norm[j],
                                  attn_k_norm[j], attn_w_o[j])
        x = x + sq_relu_mlp(rms_norm(x, mlp_norm[i]), mlp_w_up[i], mlp_w_down[i])
    return rms_norm(x, final_norm)


def setup_inputs(seed: int = 0) -> dict:
    key = jax.random.key(seed)
    ks = jax.random.split(key, 16)
    n_a = (DEPTH + N_MIXERS - 1) // N_MIXERS
    n_b = DEPTH // N_MIXERS
    f32 = jnp.float32

    def w(k, shape, fan_in):
        return jax.random.normal(k, shape, f32) * (fan_in ** -0.5)

    def gain(k, shape):
        return 1.0 + 0.02 * jax.random.normal(k, shape, f32)

    return {
        "x_prompt": jax.random.normal(ks[0], (BATCH, SEQ, D_MODEL), f32),
        "x_sample": jax.random.normal(ks[1], (DEC_BATCH, DEC_SEQ, D_MODEL), f32),
        "fourier_norm": gain(ks[2], (n_a, D_MODEL)),
        "fourier_w_out": w(ks[3], (n_a, D_MODEL, D_MODEL), D_MODEL),
        "attn_norm": gain(ks[4], (n_b, D_MODEL)),
        "attn_w_qkv": w(ks[5], (n_b, D_MODEL, QKV_DIM), D_MODEL),
        "attn_q_norm": gain(ks[6], (n_b, HEAD_DIM)),
        "attn_k_norm": gain(ks[7], (n_b, HEAD_DIM)),
        "attn_w_o": w(ks[8], (n_b, N_HEADS * HEAD_DIM, D_MODEL), N_HEADS * HEAD_DIM),
        "mlp_norm": gain(ks[9], (DEPTH, D_MODEL)),
        "mlp_w_up": w(ks[10], (DEPTH, D_MODEL, D_FF), D_MODEL),
        "mlp_w_down": w(ks[11], (DEPTH, D_FF, D_MODEL), D_FF),
        "final_norm": gain(ks[12], (D_MODEL,)),
    }


def reference(x_prompt, x_sample, fourier_norm, fourier_w_out, attn_norm, attn_w_qkv, attn_q_norm,
              attn_k_norm, attn_w_o, mlp_norm, mlp_w_up, mlp_w_down, final_norm):
    y_prompt = trunk(x_prompt, fourier_norm, fourier_w_out, attn_norm, attn_w_qkv, attn_q_norm,
                     attn_k_norm, attn_w_o, mlp_norm, mlp_w_up, mlp_w_down, final_norm)
    y_sample = trunk(x_sample, fourier_norm, fourier_w_out, attn_norm, attn_w_qkv, attn_q_norm,
                     attn_k_norm, attn_w_o, mlp_norm, mlp_w_up, mlp_w_down, final_norm)
    return (y_prompt, y_sample)
```

```python
import functools

import numpy as np
import jax
import jax.numpy as jnp
from jax import lax
from jax.experimental import pallas as pl
from jax.experimental.pallas import tpu as pltpu

F32 = jnp.float32
BF16 = jnp.bfloat16

N_FGROUPS = 8
HEAD_DIM = 128
N_HEADS = 8
N_KV_HEADS = 2
KV_GROUP = N_HEADS // N_KV_HEADS
AXIS_DIM = HEAD_DIM // 2
ROPE_THETA = 10000.0
GRID_W = 64
EPS = 1e-6

V7X_VMEM_BYTES = 64 * 1024 * 1024
VMEM_LIMIT_BYTES = V7X_VMEM_BYTES - 8 * 1024 * 1024
LANES = 128

RADIX = 4


def _params(n_grid_dims):
    return pltpu.CompilerParams(
        dimension_semantics=("arbitrary",) * n_grid_dims,
        vmem_limit_bytes=VMEM_LIMIT_BYTES,
    )


def _rms(x, g):
    return x * lax.rsqrt(jnp.mean(x * x, axis=-1, keepdims=True) + EPS) * g


def _resident(shape):
    nd = len(shape)
    return pl.BlockSpec(shape, lambda *_: (0,) * nd, pipeline_mode=pl.Buffered(1))


def _norm_cast_kernel(x_ref, g_ref, o_ref):
    o_ref[...] = _rms(x_ref[...], g_ref[...]).astype(BF16)


def _norm_cast(x2d, g, tm):
    t, d = x2d.shape
    return pl.pallas_call(
        _norm_cast_kernel,
        grid=(t // tm,),
        in_specs=[pl.BlockSpec((tm, d), lambda i: (i, 0)), _resident((1, d))],
        out_specs=pl.BlockSpec((tm, d), lambda i: (i, 0)),
        out_shape=jax.ShapeDtypeStruct((t, d), BF16),
        compiler_params=_params(1),
        name="norm_cast",
    )(x2d, g.reshape(1, d))


def _fourier_tables(s, d, tk):
    n = s // RADIX
    k = np.arange(n)
    ang = 2.0 * np.pi * ((k[:, None] * k[None, :]) % n) / n
    c = np.cos(ang).reshape(n // tk, tk, n)
    sn = np.sin(ang).reshape(n // tk, tk, n)
    dft = np.concatenate([c, sn], axis=1).astype(np.float32)
    tw = np.zeros((n, 8), np.float32)
    for j in range(1, RADIX):
        th = 2.0 * np.pi * ((k * j) % s) / s
        tw[:, 2 * (j - 1)] = np.cos(th)
        tw[:, 2 * (j - 1) + 1] = np.sin(th)
    fg = d // N_FGROUPS
    cc = np.arange(fg)
    cang = 2.0 * np.pi * ((cc[:, None] * cc[None, :]) % fg) / fg
    scale = 1.0 / np.sqrt(float(s) * fg)
    chan = np.concatenate([np.cos(cang), -np.sin(cang)], axis=0) * scale
    return dft, tw, chan.astype(np.float32)


def _fourier_kernel(dft_ref, xr_ref, tw_ref, chan_ref, w_ref, x_ref, o_ref, *, tk, d):
    e = jnp.dot(dft_ref[0], xr_ref[0], preferred_element_type=F32)
    ec = [e[:tk, j * d:(j + 1) * d] for j in range(RADIX)]
    es = [e[tk:, j * d:(j + 1) * d] for j in range(RADIX)]
    tw = tw_ref[...]
    tc, ts = [ec[0]], [es[0]]
    for j in range(1, RADIX):
        a = tw[:, 2 * (j - 1):2 * (j - 1) + 1]
        b = tw[:, 2 * (j - 1) + 1:2 * (j - 1) + 2]
        tc.append(a * ec[j] - b * es[j])
        ts.append(a * es[j] + b * ec[j])
    ac, as_ = tc[0] + tc[2], ts[0] + ts[2]
    bc, bs = tc[0] - tc[2], ts[0] - ts[2]
    cc, cs = tc[1] + tc[3], ts[1] + ts[3]
    dc, ds = tc[1] - tc[3], ts[1] - ts[3]
    uc = jnp.concatenate([ac + cc, bc - ds, ac - cc, bc + ds], axis=0).astype(BF16)
    us = jnp.concatenate([as_ + cs, bs + dc, as_ - cs, bs - dc], axis=0).astype(BF16)
    fg = d // N_FGROUPS
    chan = chan_ref[...]
    f = []
    for g in range(N_FGROUPS):
        lhs = jnp.concatenate([uc[:, g * fg:(g + 1) * fg], us[:, g * fg:(g + 1) * fg]], axis=1)
        f.append(jnp.dot(lhs, chan, preferred_element_type=F32))
    f = jnp.concatenate(f, axis=1).astype(BF16)
    y = jnp.dot(f, w_ref[...], preferred_element_type=F32)
    o_ref[0] = x_ref[0] + y.reshape(RADIX, tk, d)


def _fourier_layer(x, g, w_out):
    b, s, d = x.shape
    n = s // RADIX
    tk = min(128, n)
    tm = min(1024, s)
    xn = _norm_cast(x.reshape(b * s, d), g, tm)
    xr = xn.reshape(b, n, RADIX * d)
    dft, tw, chan = _fourier_tables(s, d, tk)
    dft = jnp.asarray(dft).astype(BF16)
    chan = jnp.asarray(chan).astype(BF16)
    x4 = x.reshape(b, RADIX, n, d)
    fg = d // N_FGROUPS
    out = pl.pallas_call(
        functools.partial(_fourier_kernel, tk=tk, d=d),
        grid=(b, n // tk),
        in_specs=[
            pl.BlockSpec((1, 2 * tk, n), lambda i, j: (j, 0, 0)),
            pl.BlockSpec((1, n, RADIX * d), lambda i, j: (i, 0, 0)),
            pl.BlockSpec((tk, 8), lambda i, j: (j, 0)),
            _resident((2 * fg, fg)),
            _resident((d, d)),
            pl.BlockSpec((1, RADIX, tk, d), lambda i, j: (i, 0, j, 0)),
        ],
        out_specs=pl.BlockSpec((1, RADIX, tk, d), lambda i, j: (i, 0, j, 0)),
        out_shape=jax.ShapeDtypeStruct((b, RADIX, n, d), F32),
        compiler_params=_params(2),
        name="fourier",
    )(dft, xr, jnp.asarray(tw), chan, w_out.astype(BF16), x4)
    return out.reshape(b, s, d)


def _mlp_kernel(*refs, has_proj, has_final, ff_chunk):
    refs = list(refs)
    x_ref = refs.pop(0)
    if has_proj:
        a_ref, wo_ref = refs.pop(0), refs.pop(0)
    g_ref, wup_ref, wdown_ref = refs.pop(0), refs.pop(0), refs.pop(0)
    if has_final:
        gf_ref = refs.pop(0)
    o_ref = refs.pop(0)

    x = x_ref[...]
    if has_proj:
        x = x + jnp.dot(a_ref[...], wo_ref[...], preferred_element_type=F32)
    xn = _rms(x, g_ref[...]).astype(BF16)
    acc = x
    for c in range(wup_ref.shape[1] // ff_chunk):
        h = jnp.dot(xn, wup_ref[:, c * ff_chunk:(c + 1) * ff_chunk], preferred_element_type=F32)
        h = jnp.maximum(h, 0.0)
        h = (h * h).astype(BF16)
        acc = acc + jnp.dot(h, wdown_ref[c * ff_chunk:(c + 1) * ff_chunk, :],
                            preferred_element_type=F32)
    if has_final:
        acc = _rms(acc, gf_ref[...])
    o_ref[...] = acc


def _mlp_layer(x2d, g, w_up, w_down, attn=None, w_o=None, g_final=None):
    t, d = x2d.shape
    ff = w_up.shape[1]
    tm = min(512, t)
    has_proj = attn is not None
    has_final = g_final is not None
    row = pl.BlockSpec((tm, d), lambda i: (i, 0))
    args, specs = [x2d], [row]
    if has_proj:
        args += [attn, w_o.astype(BF16)]
        specs += [row, _resident((d, d))]
    args += [g.reshape(1, d), w_up.astype(BF16), w_down.astype(BF16)]
    specs += [_resident((1, d)), _resident((d, ff)), _resident((ff, d))]
    if has_final:
        args.append(g_final.reshape(1, d))
        specs.append(_resident((1, d)))
    return pl.pallas_call(
        functools.partial(_mlp_kernel, has_proj=has_proj, has_final=has_final,
                          ff_chunk=min(1024, ff)),
        grid=(t // tm,),
        in_specs=specs,
        out_specs=row,
        out_shape=jax.ShapeDtypeStruct((t, d), F32),
        compiler_params=_params(1),
        name="mlp_proj" if has_proj else "mlp",
    )(*args)


def _rope_tables(s):
    pos = np.arange(s)
    inv = ROPE_THETA ** (-np.arange(0, AXIS_DIM, 2, dtype=np.float64) / AXIS_DIM)
    ang_row = (pos // GRID_W)[:, None] * inv[None, :]
    ang_col = (pos % GRID_W)[:, None] * inv[None, :]
    cos = np.concatenate([np.cos(ang_row)] * 2 + [np.cos(ang_col)] * 2, axis=1)
    sin = np.concatenate([-np.sin(ang_row), np.sin(ang_row),
                          -np.sin(ang_col), np.sin(ang_col)], axis=1)
    return cos.astype(np.float32), sin.astype(np.float32)


def _rope(x, cos, sin, first_half):
    quarter = AXIS_DIM // 2
    partner = jnp.where(first_half,
                        pltpu.roll(x, HEAD_DIM - quarter, 1),
                        pltpu.roll(x, quarter, 1))
    return x * cos + partner * sin


def _qkv_kernel(x_ref, g_ref, w_ref, gq_ref, gk_ref, cos_ref, sin_ref, q_ref, k_ref, v_ref):
    xn = _rms(x_ref[0], g_ref[...]).astype(BF16)
    qkv = jnp.dot(xn, w_ref[...], preferred_element_type=F32)
    cos, sin = cos_ref[...], sin_ref[...]
    lane = lax.broadcasted_iota(jnp.int32, cos.shape, 1)
    first_half = (lane % AXIS_DIM) < (AXIS_DIM // 2)
    for h in range(N_HEADS):
        qh = _rms(qkv[:, h * HEAD_DIM:(h + 1) * HEAD_DIM], gq_ref[...])
        q_ref[0, h] = _rope(qh, cos, sin, first_half).astype(BF16)
    for h in range(N_KV_HEADS):
        off = (N_HEADS + h) * HEAD_DIM
        kh = _rms(qkv[:, off:off + HEAD_DIM], gk_ref[...])
        k_ref[0, h] = _rope(kh, cos, sin, first_half).astype(BF16)
        off = (N_HEADS + N_KV_HEADS + h) * HEAD_DIM
        v_ref[0, h] = qkv[:, off:off + HEAD_DIM].astype(BF16)


def _qkv_layer(x, g, w_qkv, gq, gk):
    b, s, d = x.shape
    tm = min(512, s)
    cos, sin = _rope_tables(s)
    qkv_dim = w_qkv.shape[1]
    q_scale = HEAD_DIM ** -0.5 * np.log2(np.e)
    head = lambda nh: pl.BlockSpec((1, nh, tm, HEAD_DIM), lambda i, j: (i, 0, j, 0))
    tab = pl.BlockSpec((tm, HEAD_DIM), lambda i, j: (j, 0))
    return pl.pallas_call(
        _qkv_kernel,
        grid=(b, s // tm),
        in_specs=[
            pl.BlockSpec((1, tm, d), lambda i, j: (i, j, 0)),
            _resident((1, d)),
            _resident((d, qkv_dim)),
            _resident((1, HEAD_DIM)),
            _resident((1, HEAD_DIM)),
            tab, tab,
        ],
        out_specs=[head(N_HEADS), head(N_KV_HEADS), head(N_KV_HEADS)],
        out_shape=[
            jax.ShapeDtypeStruct((b, N_HEADS, s, HEAD_DIM), BF16),
            jax.ShapeDtypeStruct((b, N_KV_HEADS, s, HEAD_DIM), BF16),
            jax.ShapeDtypeStruct((b, N_KV_HEADS, s, HEAD_DIM), BF16),
        ],
        compiler_params=_params(2),
        name="qkv",
    )(x, g.reshape(1, d), w_qkv.astype(BF16), (gq * q_scale).reshape(1, HEAD_DIM),
      gk.reshape(1, HEAD_DIM), jnp.asarray(cos), jnp.asarray(sin))


def _attn_kernel(q_ref, k_ref, v_ref, o_ref):
    k = k_ref[0, 0]
    v = v_ref[0, 0]
    outs = []
    for h in range(KV_GROUP):
        s = lax.dot_general(q_ref[0, h], k, (((1,), (1,)), ((), ())),
                            preferred_element_type=F32)
        m = jnp.max(s, axis=-1, keepdims=True)
        p = jnp.exp2(s - m)
        l = jnp.sum(p, axis=-1, keepdims=True)
        o = jnp.dot(p.astype(BF16), v, preferred_element_type=F32)
        outs.append((o / l).astype(BF16))
    o_ref[0] = jnp.concatenate(outs, axis=1)


def _attention(q, k, v):
    b, _, s, _ = q.shape
    tq = min(256, s)
    kv_spec = pl.BlockSpec((1, 1, s, HEAD_DIM), lambda i, g, j: (i, g, 0, 0))
    return pl.pallas_call(
        _attn_kernel,
        grid=(b, N_KV_HEADS, s // tq),
        in_specs=[
            pl.BlockSpec((1, KV_GROUP, tq, HEAD_DIM), lambda i, g, j: (i, g, j, 0)),
            kv_spec, kv_spec,
        ],
        out_specs=pl.BlockSpec((1, tq, KV_GROUP * HEAD_DIM), lambda i, g, j: (i, j, g)),
        out_shape=jax.ShapeDtypeStruct((b, s, N_HEADS * HEAD_DIM), BF16),
        compiler_params=_params(3),
        name="attention",
    )(q, k, v)


def _trunk(x, fourier_norm, fourier_w_out, attn_norm, attn_w_qkv, attn_q_norm, attn_k_norm,
           attn_w_o, mlp_norm, mlp_w_up, mlp_w_down, final_norm):
    b, s, d = x.shape
    depth = mlp_norm.shape[0]
    for i in range(depth):
        j = i // 2
        last = i == depth - 1
        if i % 2 == 0:
            x = _fourier_layer(x, fourier_norm[j], fourier_w_out[j])
            x = _mlp_layer(x.reshape(b * s, d), mlp_norm[i], mlp_w_up[i], mlp_w_down[i],
                           g_final=final_norm if last else None).reshape(b, s, d)
        else:
            q, k, v = _qkv_layer(x, attn_norm[j], attn_w_qkv[j], attn_q_norm[j], attn_k_norm[j])
            a = _attention(q, k, v)
            x = _mlp_layer(x.reshape(b * s, d), mlp_norm[i], mlp_w_up[i], mlp_w_down[i],
                           attn=a.reshape(b * s, d), w_o=attn_w_o[j],
                           g_final=final_norm if last else None).reshape(b, s, d)
    return x


def kernel(x_prompt, x_sample, fourier_norm, fourier_w_out, attn_norm, attn_w_qkv, attn_q_norm,
           attn_k_norm, attn_w_o, mlp_norm, mlp_w_up, mlp_w_down, final_norm):
    weights = (fourier_norm, fourier_w_out, attn_norm, attn_w_qkv, attn_q_norm, attn_k_norm,
               attn_w_o, mlp_norm, mlp_w_up, mlp_w_down, final_norm)
    return (_trunk(x_prompt, *weights), _trunk(x_sample, *weights))
```

```python
import functools

import numpy as np
import jax
import jax.numpy as jnp
from jax import lax
from jax.experimental import pallas as pl
from jax.experimental.pallas import tpu as pltpu

F32 = jnp.float32
BF16 = jnp.bfloat16

N_FGROUPS = 8
HEAD_DIM = 128
N_HEADS = 8
N_KV_HEADS = 2
KV_GROUP = N_HEADS // N_KV_HEADS
AXIS_DIM = HEAD_DIM // 2
ROPE_THETA = 10000.0
GRID_W = 64
EPS = 1e-6

V7X_VMEM_BYTES = 64 * 1024 * 1024
VMEM_LIMIT_BYTES = V7X_VMEM_BYTES - 8 * 1024 * 1024
LANES = 128

RADIX = 4


def _params(n_grid_dims):
    return pltpu.CompilerParams(
        dimension_semantics=("arbitrary",) * n_grid_dims,
        vmem_limit_bytes=VMEM_LIMIT_BYTES,
    )


def _rms(x, g):
    return x * lax.rsqrt(jnp.mean(x * x, axis=-1, keepdims=True) + EPS) * g


def _resident(shape):
    nd = len(shape)
    return pl.BlockSpec(shape, lambda *_: (0,) * nd, pipeline_mode=pl.Buffered(1))


def _norm_cast_kernel(x_ref, g_ref, o_ref, cols_ref, *, d):
    rows = o_ref.shape[0]
    xn = _rms(x_ref[...], g_ref[...])
    for c in range(d // LANES):
        cols_ref[c] = xn[:, c * LANES:(c + 1) * LANES]
    for j in range(RADIX):
        for c in range(d // LANES):
            o_ref[:, j * d + c * LANES:j * d + (c + 1) * LANES] = (
                cols_ref[c, pl.ds(j, rows, stride=RADIX), :].astype(BF16))


def _norm_cast(x2d, g, tm):
    t, d = x2d.shape
    return pl.pallas_call(
        functools.partial(_norm_cast_kernel, d=d),
        grid=(t // tm,),
        in_specs=[pl.BlockSpec((tm, d), lambda i: (i, 0)), _resident((1, d))],
        out_specs=pl.BlockSpec((tm // RADIX, RADIX * d), lambda i: (i, 0)),
        out_shape=jax.ShapeDtypeStruct((t // RADIX, RADIX * d), BF16),
        scratch_shapes=[pltpu.VMEM((d // LANES, tm, LANES), F32)],
        compiler_params=_params(1),
        name="norm_cast",
    )(x2d, g.reshape(1, d))


def _fourier_tables(s, d, tk):
    n = s // RADIX
    k = np.arange(n)
    ang = 2.0 * np.pi * ((k[:, None] * k[None, :]) % n) / n
    c = np.cos(ang).reshape(n // tk, tk, n)
    sn = np.sin(ang).reshape(n // tk, tk, n)
    dft = np.concatenate([c, sn], axis=1).astype(np.float32)
    tw = np.zeros((n, 8), np.float32)
    for j in range(1, RADIX):
        th = 2.0 * np.pi * ((k * j) % s) / s
        tw[:, 2 * (j - 1)] = np.cos(th)
        tw[:, 2 * (j - 1) + 1] = np.sin(th)
    fg = d // N_FGROUPS
    cc = np.arange(fg)
    cang = 2.0 * np.pi * ((cc[:, None] * cc[None, :]) % fg) / fg
    scale = 1.0 / np.sqrt(float(s) * fg)
    chan = np.concatenate([np.cos(cang), -np.sin(cang)], axis=0) * scale
    return dft, tw, chan.astype(np.float32)


def _fourier_kernel(dft_ref, xr_ref, tw_ref, chan_ref, w_ref, x_ref, o_ref, *, tk, d):
    e = jnp.dot(dft_ref[0], xr_ref[0], preferred_element_type=F32)
    ec = [e[:tk, j * d:(j + 1) * d] for j in range(RADIX)]
    es = [e[tk:, j * d:(j + 1) * d] for j in range(RADIX)]
    tw = tw_ref[...]
    tc, ts = [ec[0]], [es[0]]
    for j in range(1, RADIX):
        a = tw[:, 2 * (j - 1):2 * (j - 1) + 1]
        b = tw[:, 2 * (j - 1) + 1:2 * (j - 1) + 2]
        tc.append(a * ec[j] - b * es[j])
        ts.append(a * es[j] + b * ec[j])
    ac, as_ = tc[0] + tc[2], ts[0] + ts[2]
    bc, bs = tc[0] - tc[2], ts[0] - ts[2]
    cc, cs = tc[1] + tc[3], ts[1] + ts[3]
    dc, ds = tc[1] - tc[3], ts[1] - ts[3]
    uc = jnp.concatenate([ac + cc, bc - ds, ac - cc, bc + ds], axis=0).astype(BF16)
    us = jnp.concatenate([as_ + cs, bs + dc, as_ - cs, bs - dc], axis=0).astype(BF16)
    fg = d // N_FGROUPS
    chan = chan_ref[...]
    f = []
    for g in range(N_FGROUPS):
        lhs = jnp.concatenate([uc[:, g * fg:(g + 1) * fg], us[:, g * fg:(g + 1) * fg]], axis=1)
        f.append(jnp.dot(lhs, chan, preferred_element_type=F32))
    f = jnp.concatenate(f, axis=1).astype(BF16)
    y = jnp.dot(f, w_ref[...], preferred_element_type=F32)
    o_ref[0] = x_ref[0] + y.reshape(RADIX, tk, d)


def _fourier_layer(x, g, w_out):
    b, s, d = x.shape
    n = s // RADIX
    tk = min(128, n)
    tm = min(1024, s)
    xr = _norm_cast(x.reshape(b * s, d), g, tm).reshape(b, n, RADIX * d)
    dft, tw, chan = _fourier_tables(s, d, tk)
    dft = jnp.asarray(dft).astype(BF16)
    chan = jnp.asarray(chan).astype(BF16)
    x4 = x.reshape(b, RADIX, n, d)
    fg = d // N_FGROUPS
    out = pl.pallas_call(
        functools.partial(_fourier_kernel, tk=tk, d=d),
        grid=(b, n // tk),
        in_specs=[
            pl.BlockSpec((1, 2 * tk, n), lambda i, j: (j, 0, 0)),
            pl.BlockSpec((1, n, RADIX * d), lambda i, j: (i, 0, 0)),
            pl.BlockSpec((tk, 8), lambda i, j: (j, 0)),
            _resident((2 * fg, fg)),
            _resident((d, d)),
            pl.BlockSpec((1, RADIX, tk, d), lambda i, j: (i, 0, j, 0)),
        ],
        out_specs=pl.BlockSpec((1, RADIX, tk, d), lambda i, j: (i, 0, j, 0)),
        out_shape=jax.ShapeDtypeStruct((b, RADIX, n, d), F32),
        compiler_params=_params(2),
        name="fourier",
    )(dft, xr, jnp.asarray(tw), chan, w_out.astype(BF16), x4)
    return out.reshape(b, s, d)


def _mlp_kernel(*refs, has_proj, has_final, ff_chunk):
    refs = list(refs)
    x_ref = refs.pop(0)
    if has_proj:
        a_ref, wo_ref = refs.pop(0), refs.pop(0)
    g_ref, wup_ref, wdown_ref = refs.pop(0), refs.pop(0), refs.pop(0)
    if has_final:
        gf_ref = refs.pop(0)
    o_ref = refs.pop(0)

    x = x_ref[...]
    if has_proj:
        x = x + jnp.dot(a_ref[...], wo_ref[...], preferred_element_type=F32)
    xn = _rms(x, g_ref[...]).astype(BF16)
    acc = x
    for c in range(wup_ref.shape[1] // ff_chunk):
        h = jnp.dot(xn, wup_ref[:, c * ff_chunk:(c + 1) * ff_chunk], preferred_element_type=F32)
        h = jnp.maximum(h, 0.0)
        h = (h * h).astype(BF16)
        acc = acc + jnp.dot(h, wdown_ref[c * ff_chunk:(c + 1) * ff_chunk, :],
                            preferred_element_type=F32)
    if has_final:
        acc = _rms(acc, gf_ref[...])
    o_ref[...] = acc


def _mlp_layer(x2d, g, w_up, w_down, attn=None, w_o=None, g_final=None):
    t, d = x2d.shape
    ff = w_up.shape[1]
    tm = min(512, t)
    has_proj = attn is not None
    has_final = g_final is not None
    row = pl.BlockSpec((tm, d), lambda i: (i, 0))
    args, specs = [x2d], [row]
    if has_proj:
        args += [attn, w_o.astype(BF16)]
        specs += [row, _resident((d, d))]
    args += [g.reshape(1, d), w_up.astype(BF16), w_down.astype(BF16)]
    specs += [_resident((1, d)), _resident((d, ff)), _resident((ff, d))]
    if has_final:
        args.append(g_final.reshape(1, d))
        specs.append(_resident((1, d)))
    return pl.pallas_call(
        functools.partial(_mlp_kernel, has_proj=has_proj, has_final=has_final,
                          ff_chunk=min(1024, ff)),
        grid=(t // tm,),
        in_specs=specs,
        out_specs=row,
        out_shape=jax.ShapeDtypeStruct((t, d), F32),
        compiler_params=_params(1),
        name="mlp_proj" if has_proj else "mlp",
    )(*args)


def _head_lane_order():
    quarter = AXIS_DIM // 2
    blocks = np.arange(HEAD_DIM).reshape(4, quarter)
    return np.concatenate([blocks[0], blocks[2], blocks[1], blocks[3]])


def _rope_tables(s):
    pos = np.arange(s)
    inv = ROPE_THETA ** (-np.arange(0, AXIS_DIM, 2, dtype=np.float64) / AXIS_DIM)
    ang = np.concatenate([(pos // GRID_W)[:, None] * inv[None, :],
                          (pos % GRID_W)[:, None] * inv[None, :]], axis=1)
    cos = np.concatenate([np.cos(ang), np.cos(ang)], axis=1)
    sin = np.concatenate([-np.sin(ang), np.sin(ang)], axis=1)
    return cos.astype(np.float32), sin.astype(np.float32)


def _rope(x, cos, sin):
    return x * cos + pltpu.roll(x, HEAD_DIM // 2, 1) * sin


def _qkv_kernel(x_ref, g_ref, w_ref, gq_ref, gk_ref, cos_ref, sin_ref, q_ref, k_ref, v_ref):
    xn = _rms(x_ref[0], g_ref[...]).astype(BF16)
    qkv = jnp.dot(xn, w_ref[...], preferred_element_type=F32)
    cos, sin = cos_ref[...], sin_ref[...]
    for h in range(N_HEADS):
        qh = _rms(qkv[:, h * HEAD_DIM:(h + 1) * HEAD_DIM], gq_ref[...])
        q_ref[0, h] = _rope(qh, cos, sin).astype(BF16)
    lane = lax.broadcasted_iota(jnp.int32, cos.shape, 1)
    ones_col = jnp.where(lane == 0, 1.0, 0.0).astype(BF16)
    for h in range(N_KV_HEADS):
        off = (N_HEADS + h) * HEAD_DIM
        kh = _rms(qkv[:, off:off + HEAD_DIM], gk_ref[...])
        k_ref[0, h] = _rope(kh, cos, sin).astype(BF16)
        off = (N_HEADS + N_KV_HEADS + h) * HEAD_DIM
        v_ref[0, h, :, :HEAD_DIM] = qkv[:, off:off + HEAD_DIM].astype(BF16)
        v_ref[0, h, :, HEAD_DIM:] = ones_col


def _qkv_layer(x, g, w_qkv, gq, gk):
    b, s, d = x.shape
    tm = min(512, s)
    cos, sin = _rope_tables(s)
    qkv_dim = w_qkv.shape[1]
    n_qk = (N_HEADS + N_KV_HEADS) * HEAD_DIM
    order = _head_lane_order()
    heads = np.arange(N_HEADS + N_KV_HEADS)[:, None] * HEAD_DIM
    cols = np.concatenate([(heads + order[None, :]).ravel(), np.arange(n_qk, qkv_dim)])
    w = w_qkv[:, cols].astype(BF16)
    q_scale = HEAD_DIM ** -0.5 * np.log2(np.e)
    head = lambda nh, width: pl.BlockSpec((1, nh, tm, width), lambda i, j: (i, 0, j, 0))
    tab = pl.BlockSpec((tm, HEAD_DIM), lambda i, j: (j, 0))
    return pl.pallas_call(
        _qkv_kernel,
        grid=(b, s // tm),
        in_specs=[
            pl.BlockSpec((1, tm, d), lambda i, j: (i, j, 0)),
            _resident((1, d)),
            _resident((d, qkv_dim)),
            _resident((1, HEAD_DIM)),
            _resident((1, HEAD_DIM)),
            tab, tab,
        ],
        out_specs=[head(N_HEADS, HEAD_DIM), head(N_KV_HEADS, HEAD_DIM),
                   head(N_KV_HEADS, 2 * HEAD_DIM)],
        out_shape=[
            jax.ShapeDtypeStruct((b, N_HEADS, s, HEAD_DIM), BF16),
            jax.ShapeDtypeStruct((b, N_KV_HEADS, s, HEAD_DIM), BF16),
            jax.ShapeDtypeStruct((b, N_KV_HEADS, s, 2 * HEAD_DIM), BF16),
        ],
        compiler_params=_params(2),
        name="qkv",
    )(x, g.reshape(1, d), w, (gq[order] * q_scale).reshape(1, HEAD_DIM),
      gk[order].reshape(1, HEAD_DIM), jnp.asarray(cos), jnp.asarray(sin))


def _attn_kernel(q_ref, k_ref, v_ref, o_ref, s0, s1, p0, p1, *, tq, nq):
    s_buf, p_buf = (s0, s1), (p0, p1)
    seq = k_ref.shape[2]

    def scores(row, h, slot):
        q = q_ref[0, h, pl.ds(row, tq), :]
        s_buf[slot][...] = lax.dot_general(q, k_ref[0, 0], (((1,), (1,)), ((), ())),
                                           preferred_element_type=F32)

    def softmax(slot):
        sb, pb = s_buf[slot], p_buf[slot]
        m = sb[:, 0:LANES]
        for c in range(1, seq // LANES):
            m = jnp.maximum(m, sb[:, c * LANES:(c + 1) * LANES])
        mb = jnp.broadcast_to(jnp.max(m, axis=-1, keepdims=True), (tq, LANES))
        for c in range(seq // LANES):
            p = jnp.exp2(sb[:, c * LANES:(c + 1) * LANES] - mb)
            pb[:, c * LANES:(c + 1) * LANES] = p.astype(BF16)

    def weighted_values(row, h, slot):
        o = jnp.dot(p_buf[slot][...], v_ref[0, 0], preferred_element_type=F32)
        o = o[:, :HEAD_DIM] / o[:, HEAD_DIM:HEAD_DIM + 1]
        o_ref[0, pl.ds(row, tq), h * HEAD_DIM:(h + 1) * HEAD_DIM] = o.astype(BF16)

    p1[...] = jnp.ones_like(p1)
    scores(0, 0, 0)

    def body(qi, carry):
        row = pl.multiple_of(qi * tq, tq)
        row_next = pl.multiple_of(jnp.minimum(qi + 1, nq - 1) * tq, tq)
        row_prev = pl.multiple_of(jnp.maximum(qi - 1, 0) * tq, tq)
        for h in range(KV_GROUP):
            slot = h % 2
            if h + 1 < KV_GROUP:
                scores(row, h + 1, 1 - slot)
            else:
                scores(row_next, 0, 1 - slot)
            softmax(slot)
            if h > 0:
                weighted_values(row, h - 1, 1 - slot)
            else:
                weighted_values(row_prev, KV_GROUP - 1, 1 - slot)
        return carry

    lax.fori_loop(0, nq, body, 0)
    weighted_values((nq - 1) * tq, KV_GROUP - 1, 1)


def _attention(q, k, v):
    b, _, s, _ = q.shape
    tq = min(256, s)
    nq = s // tq
    whole = lambda nh, width: pl.BlockSpec((1, nh, s, width), lambda i, g: (i, g, 0, 0))
    return pl.pallas_call(
        functools.partial(_attn_kernel, tq=tq, nq=nq),
        grid=(b, N_KV_HEADS),
        in_specs=[whole(KV_GROUP, HEAD_DIM), whole(1, HEAD_DIM), whole(1, 2 * HEAD_DIM)],
        out_specs=pl.BlockSpec((1, s, KV_GROUP * HEAD_DIM), lambda i, g: (i, 0, g)),
        out_shape=jax.ShapeDtypeStruct((b, s, N_HEADS * HEAD_DIM), BF16),
        scratch_shapes=[pltpu.VMEM((tq, s), F32), pltpu.VMEM((tq, s), F32),
                        pltpu.VMEM((tq, s), BF16), pltpu.VMEM((tq, s), BF16)],
        compiler_params=_params(2),
        name="attention",
    )(q, k, v)


def _trunk(x, fourier_norm, fourier_w_out, attn_norm, attn_w_qkv, attn_q_norm, attn_k_norm,
           attn_w_o, mlp_norm, mlp_w_up, mlp_w_down, final_norm):
    b, s, d = x.shape
    depth = mlp_norm.shape[0]
    for i in range(depth):
        j = i // 2
        last = i == depth - 1
        if i % 2 == 0:
            x = _fourier_layer(x, fourier_norm[j], fourier_w_out[j])
            x = _mlp_layer(x.reshape(b * s, d), mlp_norm[i], mlp_w_up[i], mlp_w_down[i],
                           g_final=final_norm if last else None).reshape(b, s, d)
        else:
            q, k, v = _qkv_layer(x, attn_norm[j], attn_w_qkv[j], attn_q_norm[j], attn_k_norm[j])
            a = _attention(q, k, v)
            x = _mlp_layer(x.reshape(b * s, d), mlp_norm[i], mlp_w_up[i], mlp_w_down[i],
                           attn=a.reshape(b * s, d), w_o=attn_w_o[j],
                           g_final=final_norm if last else None).reshape(b, s, d)
    return x


def kernel(x_prompt, x_sample, fourier_norm, fourier_w_out, attn_norm, attn_w_qkv, attn_q_norm,
           attn_k_norm, attn_w_o, mlp_norm, mlp_w_up, mlp_w_down, final_norm):
    weights = (fourier_norm, fourier_w_out, attn_norm, attn_w_qkv, attn_q_norm, attn_k_norm,
               attn_w_o, mlp_norm, mlp_w_up, mlp_w_down, final_norm)
    return (_trunk(x_prompt, *weights), _trunk(x_sample, *weights))
```

```python
import functools

import numpy as np
import jax
import jax.numpy as jnp
from jax import lax
from jax.experimental import pallas as pl
from jax.experimental.pallas import tpu as pltpu

F32 = jnp.float32
BF16 = jnp.bfloat16

N_FGROUPS = 8
HEAD_DIM = 128
N_HEADS = 8
N_KV_HEADS = 2
KV_GROUP = N_HEADS // N_KV_HEADS
AXIS_DIM = HEAD_DIM // 2
ROPE_THETA = 10000.0
GRID_W = 64
EPS = 1e-6

V7X_VMEM_BYTES = 64 * 1024 * 1024
VMEM_LIMIT_BYTES = V7X_VMEM_BYTES - 8 * 1024 * 1024
LANES = 128

RADIX = 4


def _params(n_grid_dims):
    return pltpu.CompilerParams(
        dimension_semantics=("arbitrary",) * n_grid_dims,
        vmem_limit_bytes=VMEM_LIMIT_BYTES,
    )


def _rms(x, g):
    return x * lax.rsqrt(jnp.mean(x * x, axis=-1, keepdims=True) + EPS) * g


def _resident(shape):
    nd = len(shape)
    return pl.BlockSpec(shape, lambda *_: (0,) * nd, pipeline_mode=pl.Buffered(1))


def _norm_cast_kernel(x_ref, g_ref, o_ref, cols_ref, *, d):
    rows = o_ref.shape[0]
    xn = _rms(x_ref[...], g_ref[...])
    for c in range(d // LANES):
        cols_ref[c] = xn[:, c * LANES:(c + 1) * LANES]
    for j in range(RADIX):
        for c in range(d // LANES):
            o_ref[:, j * d + c * LANES:j * d + (c + 1) * LANES] = (
                cols_ref[c, pl.ds(j, rows, stride=RADIX), :].astype(BF16))


def _norm_cast(x2d, g, tm):
    t, d = x2d.shape
    return pl.pallas_call(
        functools.partial(_norm_cast_kernel, d=d),
        grid=(t // tm,),
        in_specs=[pl.BlockSpec((tm, d), lambda i: (i, 0)), _resident((1, d))],
        out_specs=pl.BlockSpec((tm // RADIX, RADIX * d), lambda i: (i, 0)),
        out_shape=jax.ShapeDtypeStruct((t // RADIX, RADIX * d), BF16),
        scratch_shapes=[pltpu.VMEM((d // LANES, tm, LANES), F32)],
        compiler_params=_params(1),
        name="norm_cast",
    )(x2d, g.reshape(1, d))


def _fourier_tables(s, d, tk):
    n = s // RADIX
    k = np.arange(n)
    ang = 2.0 * np.pi * ((k[:, None] * k[None, :]) % n) / n
    c = np.cos(ang).reshape(n // tk, tk, n)
    sn = np.sin(ang).reshape(n // tk, tk, n)
    dft = np.concatenate([c, sn], axis=1).astype(np.float32)
    tw = np.zeros((n, 8), np.float32)
    for j in range(1, RADIX):
        th = 2.0 * np.pi * ((k * j) % s) / s
        tw[:, 2 * (j - 1)] = np.cos(th)
        tw[:, 2 * (j - 1) + 1] = np.sin(th)
    fg = d // N_FGROUPS
    cc = np.arange(fg)
    cang = 2.0 * np.pi * ((cc[:, None] * cc[None, :]) % fg) / fg
    scale = 1.0 / np.sqrt(float(s) * fg)
    chan = np.concatenate([np.cos(cang), -np.sin(cang)], axis=0) * scale
    return dft, tw, chan.astype(np.float32)


def _fourier_kernel(dft_ref, xr_ref, tw_ref, chan_ref, w_ref, x_ref, o_ref, *, tk, d):
    e = jnp.dot(dft_ref[0], xr_ref[0], preferred_element_type=F32)
    ec = [e[:tk, j * d:(j + 1) * d] for j in range(RADIX)]
    es = [e[tk:, j * d:(j + 1) * d] for j in range(RADIX)]
    tw = tw_ref[...]
    tc, ts = [ec[0]], [es[0]]
    for j in range(1, RADIX):
        a = tw[:, 2 * (j - 1):2 * (j - 1) + 1]
        b = tw[:, 2 * (j - 1) + 1:2 * (j - 1) + 2]
        tc.append(a * ec[j] - b * es[j])
        ts.append(a * es[j] + b * ec[j])
    ac, as_ = tc[0] + tc[2], ts[0] + ts[2]
    bc, bs = tc[0] - tc[2], ts[0] - ts[2]
    cc, cs = tc[1] + tc[3], ts[1] + ts[3]
    dc, ds = tc[1] - tc[3], ts[1] - ts[3]
    uc = jnp.concatenate([ac + cc, bc - ds, ac - cc, bc + ds], axis=0).astype(BF16)
    us = jnp.concatenate([as_ + cs, bs + dc, as_ - cs, bs - dc], axis=0).astype(BF16)
    fg = d // N_FGROUPS
    chan = chan_ref[...]
    f = []
    for g in range(N_FGROUPS):
        lhs = jnp.concatenate([uc[:, g * fg:(g + 1) * fg], us[:, g * fg:(g + 1) * fg]], axis=1)
        f.append(jnp.dot(lhs, chan, preferred_element_type=F32))
    f = jnp.concatenate(f, axis=1).astype(BF16)
    y = jnp.dot(f, w_ref[...], preferred_element_type=F32)
    o_ref[0] = x_ref[0] + y.reshape(RADIX, tk, d)


def _fourier_layer(x, g, w_out):
    b, s, d = x.shape
    n = s // RADIX
    tk = min(128, n)
    tm = min(1024, s)
    xr = _norm_cast(x.reshape(b * s, d), g, tm).reshape(b, n, RADIX * d)
    dft, tw, chan = _fourier_tables(s, d, tk)
    dft = jnp.asarray(dft).astype(BF16)
    chan = jnp.asarray(chan).astype(BF16)
    x4 = x.reshape(b, RADIX, n, d)
    fg = d // N_FGROUPS
    out = pl.pallas_call(
        functools.partial(_fourier_kernel, tk=tk, d=d),
        grid=(b, n // tk),
        in_specs=[
            pl.BlockSpec((1, 2 * tk, n), lambda i, j: (j, 0, 0)),
            pl.BlockSpec((1, n, RADIX * d), lambda i, j: (i, 0, 0)),
            pl.BlockSpec((tk, 8), lambda i, j: (j, 0)),
            _resident((2 * fg, fg)),
            _resident((d, d)),
            pl.BlockSpec((1, RADIX, tk, d), lambda i, j: (i, 0, j, 0)),
        ],
        out_specs=pl.BlockSpec((1, RADIX, tk, d), lambda i, j: (i, 0, j, 0)),
        out_shape=jax.ShapeDtypeStruct((b, RADIX, n, d), F32),
        compiler_params=_params(2),
        name="fourier",
    )(dft, xr, jnp.asarray(tw), chan, w_out.astype(BF16), x4)
    return out.reshape(b, s, d)


def _head_lane_order():
    quarter = AXIS_DIM // 2
    blocks = np.arange(HEAD_DIM).reshape(4, quarter)
    return np.concatenate([blocks[0], blocks[2], blocks[1], blocks[3]])


def _rope_tables(s):
    pos = np.arange(s)
    inv = ROPE_THETA ** (-np.arange(0, AXIS_DIM, 2, dtype=np.float64) / AXIS_DIM)
    ang = np.concatenate([(pos // GRID_W)[:, None] * inv[None, :],
                          (pos % GRID_W)[:, None] * inv[None, :]], axis=1)
    cos = np.concatenate([np.cos(ang), np.cos(ang)], axis=1)
    sin = np.concatenate([-np.sin(ang), np.sin(ang)], axis=1)
    return cos.astype(np.float32), sin.astype(np.float32)


def _rope(x, cos, sin):
    return x * cos + pltpu.roll(x, HEAD_DIM // 2, 1) * sin


def _store_heads(raw, gq_ref, gk_ref, cos_ref, sin_ref, q_ref, k_ref, v_ref):
    cos, sin = cos_ref[...], sin_ref[...]
    for h in range(N_HEADS):
        qh = _rms(raw[:, h * HEAD_DIM:(h + 1) * HEAD_DIM], gq_ref[...])
        q_ref[0, h] = _rope(qh, cos, sin).astype(BF16)
    lane = lax.broadcasted_iota(jnp.int32, cos.shape, 1)
    ones_col = jnp.where(lane == 0, 1.0, 0.0).astype(BF16)
    for h in range(N_KV_HEADS):
        off = (N_HEADS + h) * HEAD_DIM
        kh = _rms(raw[:, off:off + HEAD_DIM], gk_ref[...])
        k_ref[0, h] = _rope(kh, cos, sin).astype(BF16)
        off = (N_HEADS + N_KV_HEADS + h) * HEAD_DIM
        v_ref[0, h, :, :HEAD_DIM] = raw[:, off:off + HEAD_DIM].astype(BF16)
        v_ref[0, h, :, HEAD_DIM:] = ones_col


def _mlp_block(x, g_ref, wup_ref, wdown_ref, ff_chunk):
    xn = _rms(x, g_ref[...]).astype(BF16)
    acc = x
    for c in range(wup_ref.shape[1] // ff_chunk):
        h = jnp.dot(xn, wup_ref[:, c * ff_chunk:(c + 1) * ff_chunk], preferred_element_type=F32)
        h = jnp.maximum(h, 0.0)
        h = (h * h).astype(BF16)
        acc = acc + jnp.dot(h, wdown_ref[c * ff_chunk:(c + 1) * ff_chunk, :],
                            preferred_element_type=F32)
    return acc


def _mlp_kernel(*refs, has_proj, has_final, ff_chunk):
    refs = list(refs)
    x_ref = refs.pop(0)
    if has_proj:
        a_ref, wo_ref = refs.pop(0), refs.pop(0)
    g_ref, wup_ref, wdown_ref = refs.pop(0), refs.pop(0), refs.pop(0)
    if has_final:
        gf_ref = refs.pop(0)
    o_ref = refs.pop(0)

    x = x_ref[...]
    if has_proj:
        x = x + jnp.dot(a_ref[...], wo_ref[...], preferred_element_type=F32)
    acc = _mlp_block(x, g_ref, wup_ref, wdown_ref, ff_chunk)
    if has_final:
        acc = _rms(acc, gf_ref[...])
    o_ref[...] = acc


def _mlp_layer(x2d, g, w_up, w_down, attn=None, w_o=None, g_final=None):
    t, d = x2d.shape
    ff = w_up.shape[1]
    tm = min(512, t)
    has_proj = attn is not None
    has_final = g_final is not None
    row = pl.BlockSpec((tm, d), lambda i: (i, 0))
    args, specs = [x2d], [row]
    if has_proj:
        args += [attn, w_o.astype(BF16)]
        specs += [row, _resident((d, d))]
    args += [g.reshape(1, d), w_up.astype(BF16), w_down.astype(BF16)]
    specs += [_resident((1, d)), _resident((d, ff)), _resident((ff, d))]
    if has_final:
        args.append(g_final.reshape(1, d))
        specs.append(_resident((1, d)))
    return pl.pallas_call(
        functools.partial(_mlp_kernel, has_proj=has_proj, has_final=has_final,
                          ff_chunk=min(1024, ff)),
        grid=(t // tm,),
        in_specs=specs,
        out_specs=row,
        out_shape=jax.ShapeDtypeStruct((t, d), F32),
        compiler_params=_params(1),
        name="mlp_proj" if has_proj else "mlp",
    )(*args)


def _mlp_qkv_kernel(x_ref, g_ref, wup_ref, wdown_ref, ga_ref, wqkv_ref, gq_ref, gk_ref,
                    cos_ref, sin_ref, o_ref, q_ref, k_ref, v_ref, raw_ref, *, ff_chunk):
    i = pl.program_id(0)
    slot = i % 2

    @pl.when(i == 0)
    def _():
        raw_ref[1] = jnp.zeros(raw_ref.shape[1:], F32)

    _store_heads(raw_ref.at[1 - slot], gq_ref, gk_ref, cos_ref, sin_ref, q_ref, k_ref, v_ref)
    acc = _mlp_block(x_ref[...], g_ref, wup_ref, wdown_ref, ff_chunk)
    o_ref[...] = acc
    xn = _rms(acc, ga_ref[...]).astype(BF16)
    raw_ref[slot] = jnp.dot(xn, wqkv_ref[...], preferred_element_type=F32)


def _mlp_qkv_layer(x, g, w_up, w_down, g_attn, w_qkv, gq, gk):
    b, s, d = x.shape
    ff = w_up.shape[1]
    tm = min(512, s)
    per_b = s // tm
    nt = b * per_b
    cos, sin = _rope_tables(s)
    qkv_dim = w_qkv.shape[1]
    n_qk = (N_HEADS + N_KV_HEADS) * HEAD_DIM
    order = _head_lane_order()
    heads = np.arange(N_HEADS + N_KV_HEADS)[:, None] * HEAD_DIM
    cols = np.concatenate([(heads + order[None, :]).ravel(), np.arange(n_qk, qkv_dim)])
    w = w_qkv[:, cols].astype(BF16)
    q_scale = HEAD_DIM ** -0.5 * np.log2(np.e)
    cur = lambda i: jnp.minimum(i, nt - 1)
    prev = lambda i: jnp.maximum(i - 1, 0)
    row = pl.BlockSpec((tm, d), lambda i: (cur(i), 0))
    head = lambda nh, width: pl.BlockSpec(
        (1, nh, tm, width), lambda i: (prev(i) // per_b, 0, prev(i) % per_b, 0))
    tab = pl.BlockSpec((tm, HEAD_DIM), lambda i: (prev(i) % per_b, 0))
    x2, q, k, v = pl.pallas_call(
        functools.partial(_mlp_qkv_kernel, ff_chunk=min(1024, ff)),
        grid=(nt + 1,),
        in_specs=[
            row, _resident((1, d)), _resident((d, ff)), _resident((ff, d)),
            _resident((1, d)), _resident((d, qkv_dim)),
            _resident((1, HEAD_DIM)), _resident((1, HEAD_DIM)), tab, tab,
        ],
        out_specs=[row, head(N_HEADS, HEAD_DIM), head(N_KV_HEADS, HEAD_DIM),
                   head(N_KV_HEADS, 2 * HEAD_DIM)],
        out_shape=[
            jax.ShapeDtypeStruct((b * s, d), F32),
            jax.ShapeDtypeStruct((b, N_HEADS, s, HEAD_DIM), BF16),
            jax.ShapeDtypeStruct((b, N_KV_HEADS, s, HEAD_DIM), BF16),
            jax.ShapeDtypeStruct((b, N_KV_HEADS, s, 2 * HEAD_DIM), BF16),
        ],
        scratch_shapes=[pltpu.VMEM((2, tm, qkv_dim), F32)],
        compiler_params=_params(1),
        name="mlp_qkv",
    )(x.reshape(b * s, d), g.reshape(1, d), w_up.astype(BF16), w_down.astype(BF16),
      g_attn.reshape(1, d), w, (gq[order] * q_scale).reshape(1, HEAD_DIM),
      gk[order].reshape(1, HEAD_DIM), jnp.asarray(cos), jnp.asarray(sin))
    return x2.reshape(b, s, d), q, k, v


def _attn_kernel(q_ref, k_ref, v_ref, o_ref, s0, s1, p0, p1, *, tq, nq):
    s_buf, p_buf = (s0, s1), (p0, p1)
    seq = k_ref.shape[2]

    def scores(row, h, slot):
        q = q_ref[0, h, pl.ds(row, tq), :]
        s_buf[slot][:, :seq] = lax.dot_general(q, k_ref[0, 0], (((1,), (1,)), ((), ())),
                                               preferred_element_type=F32)

    def softmax(slot):
        sb, pb = s_buf[slot], p_buf[slot]
        m = sb[:, 0:LANES]
        for c in range(1, seq // LANES):
            m = jnp.maximum(m, sb[:, c * LANES:(c + 1) * LANES])
        mb = jnp.broadcast_to(jnp.max(m, axis=-1, keepdims=True), (tq, LANES))
        for c in range(seq // LANES):
            p = jnp.exp2(sb[:, c * LANES:(c + 1) * LANES] - mb)
            pb[:, c * LANES:(c + 1) * LANES] = p.astype(BF16)

    def weighted_values(row, h, slot):
        o = jnp.dot(p_buf[slot][:, :seq], v_ref[0, 0], preferred_element_type=F32)
        o = o[:, :HEAD_DIM] / o[:, HEAD_DIM:HEAD_DIM + 1]
        o_ref[0, pl.ds(row, tq), h * HEAD_DIM:(h + 1) * HEAD_DIM] = o.astype(BF16)

    p1[...] = jnp.ones_like(p1)
    scores(0, 0, 0)

    def body(qi, carry):
        row = pl.multiple_of(qi * tq, tq)
        row_next = pl.multiple_of(jnp.minimum(qi + 1, nq - 1) * tq, tq)
        row_prev = pl.multiple_of(jnp.maximum(qi - 1, 0) * tq, tq)
        for h in range(KV_GROUP):
            slot = h % 2
            if h + 1 < KV_GROUP:
                scores(row, h + 1, 1 - slot)
            else:
                scores(row_next, 0, 1 - slot)
            softmax(slot)
            if h > 0:
                weighted_values(row, h - 1, 1 - slot)
            else:
                weighted_values(row_prev, KV_GROUP - 1, 1 - slot)
        return carry

    lax.fori_loop(0, nq, body, 0)
    weighted_values((nq - 1) * tq, KV_GROUP - 1, 1)


def _attention(q, k, v):
    b, _, s, _ = q.shape
    tq = min(256, s)
    nq = s // tq
    whole = lambda nh, width: pl.BlockSpec((1, nh, s, width), lambda i, g: (i, g, 0, 0))
    return pl.pallas_call(
        functools.partial(_attn_kernel, tq=tq, nq=nq),
        grid=(b, N_KV_HEADS),
        in_specs=[whole(KV_GROUP, HEAD_DIM), whole(1, HEAD_DIM), whole(1, 2 * HEAD_DIM)],
        out_specs=pl.BlockSpec((1, s, KV_GROUP * HEAD_DIM), lambda i, g: (i, 0, g)),
        out_shape=jax.ShapeDtypeStruct((b, s, N_HEADS * HEAD_DIM), BF16),
        scratch_shapes=[pltpu.VMEM((tq, s + LANES), F32), pltpu.VMEM((tq, s + LANES), F32),
                        pltpu.VMEM((tq, s + LANES), BF16), pltpu.VMEM((tq, s + LANES), BF16)],
        compiler_params=_params(2),
        name="attention",
    )(q, k, v)


def _trunk(x, fourier_norm, fourier_w_out, attn_norm, attn_w_qkv, attn_q_norm, attn_k_norm,
           attn_w_o, mlp_norm, mlp_w_up, mlp_w_down, final_norm):
    b, s, d = x.shape
    depth = mlp_norm.shape[0]
    qkv = None
    for i in range(depth):
        j = i // 2
        g_final = final_norm if i == depth - 1 else None
        if i % 2 == 0:
            x = _fourier_layer(x, fourier_norm[j], fourier_w_out[j])
            if i + 1 < depth:
                x, *qkv = _mlp_qkv_layer(x, mlp_norm[i], mlp_w_up[i], mlp_w_down[i], attn_norm[j],
                                         attn_w_qkv[j], attn_q_norm[j], attn_k_norm[j])
            else:
                x = _mlp_layer(x.reshape(b * s, d), mlp_norm[i], mlp_w_up[i], mlp_w_down[i],
                               g_final=g_final).reshape(b, s, d)
        else:
            a = _attention(*qkv)
            x = _mlp_layer(x.reshape(b * s, d), mlp_norm[i], mlp_w_up[i], mlp_w_down[i],
                           attn=a.reshape(b * s, d), w_o=attn_w_o[j],
                           g_final=g_final).reshape(b, s, d)
    return x


def kernel(x_prompt, x_sample, fourier_norm, fourier_w_out, attn_norm, attn_w_qkv, attn_q_norm,
           attn_k_norm, attn_w_o, mlp_norm, mlp_w_up, mlp_w_down, final_norm):
    weights = (fourier_norm, fourier_w_out, attn_norm, attn_w_qkv, attn_q_norm, attn_k_norm,
               attn_w_o, mlp_norm, mlp_w_up, mlp_w_down, final_norm)
    return (_trunk(x_prompt, *weights), _trunk(x_sample, *weights))
```

```python
import functools

import numpy as np
import jax
import jax.numpy as jnp
from jax import lax
from jax.experimental import pallas as pl
from jax.experimental.pallas import tpu as pltpu

F32 = jnp.float32
BF16 = jnp.bfloat16

N_FGROUPS = 8
HEAD_DIM = 128
N_HEADS = 8
N_KV_HEADS = 2
KV_GROUP = N_HEADS // N_KV_HEADS
AXIS_DIM = HEAD_DIM // 2
ROPE_THETA = 10000.0
GRID_W = 64
EPS = 1e-6

V7X_VMEM_BYTES = 64 * 1024 * 1024
VMEM_LIMIT_BYTES = V7X_VMEM_BYTES - 8 * 1024 * 1024
LANES = 128

RADIX = 4
TILES_PER_TRIP = 2


def _params(n_grid_dims):
    return pltpu.CompilerParams(
        dimension_semantics=("arbitrary",) * n_grid_dims,
        vmem_limit_bytes=VMEM_LIMIT_BYTES,
    )


def _rms(x, g):
    return x * lax.rsqrt(jnp.mean(x * x, axis=-1, keepdims=True) + EPS) * g


def _resident(shape):
    nd = len(shape)
    return pl.BlockSpec(shape, lambda *_: (0,) * nd, pipeline_mode=pl.Buffered(1))


def _norm_cast_kernel(x_ref, g_ref, o_ref, cols_ref, *, d):
    rows = o_ref.shape[0]
    xn = _rms(x_ref[...], g_ref[...])
    for c in range(d // LANES):
        cols_ref[c] = xn[:, c * LANES:(c + 1) * LANES]
    for j in range(RADIX):
        for c in range(d // LANES):
            o_ref[:, j * d + c * LANES:j * d + (c + 1) * LANES] = (
                cols_ref[c, pl.ds(j, rows, stride=RADIX), :].astype(BF16))


def _norm_cast(x2d, g, tm):
    t, d = x2d.shape
    return pl.pallas_call(
        functools.partial(_norm_cast_kernel, d=d),
        grid=(t // tm,),
        in_specs=[pl.BlockSpec((tm, d), lambda i: (i, 0)), _resident((1, d))],
        out_specs=pl.BlockSpec((tm // RADIX, RADIX * d), lambda i: (i, 0)),
        out_shape=jax.ShapeDtypeStruct((t // RADIX, RADIX * d), BF16),
        scratch_shapes=[pltpu.VMEM((d // LANES, tm, LANES), F32)],
        compiler_params=_params(1),
        name="norm_cast",
    )(x2d, g.reshape(1, d))


def _fourier_tables(s, d, tk):
    n = s // RADIX
    k = np.arange(n)
    ang = 2.0 * np.pi * ((k[:, None] * k[None, :]) % n) / n
    c = np.cos(ang).reshape(n // tk, tk, n)
    sn = np.sin(ang).reshape(n // tk, tk, n)
    dft = np.concatenate([c, sn], axis=1).astype(np.float32)
    tw = np.zeros((n, 8), np.float32)
    for j in range(1, RADIX):
        th = 2.0 * np.pi * ((k * j) % s) / s
        tw[:, 2 * (j - 1)] = np.cos(th)
        tw[:, 2 * (j - 1) + 1] = np.sin(th)
    fg = d // N_FGROUPS
    cc = np.arange(fg)
    cang = 2.0 * np.pi * ((cc[:, None] * cc[None, :]) % fg) / fg
    scale = 1.0 / np.sqrt(float(s) * fg)
    chan = np.concatenate([np.cos(cang), -np.sin(cang)], axis=0) * scale
    return dft, tw, chan.astype(np.float32)


def _fourier_kernel(dft_ref, xr_ref, tw_ref, chan_ref, w_ref, x_ref, o_ref, *, tk, d):
    e = jnp.dot(dft_ref[0], xr_ref[0], preferred_element_type=F32)
    ec = [e[:tk, j * d:(j + 1) * d] for j in range(RADIX)]
    es = [e[tk:, j * d:(j + 1) * d] for j in range(RADIX)]
    tw = tw_ref[...]
    tc, ts = [ec[0]], [es[0]]
    for j in range(1, RADIX):
        a = tw[:, 2 * (j - 1):2 * (j - 1) + 1]
        b = tw[:, 2 * (j - 1) + 1:2 * (j - 1) + 2]
        tc.append(a * ec[j] - b * es[j])
        ts.append(a * es[j] + b * ec[j])
    ac, as_ = tc[0] + tc[2], ts[0] + ts[2]
    bc, bs = tc[0] - tc[2], ts[0] - ts[2]
    cc, cs = tc[1] + tc[3], ts[1] + ts[3]
    dc, ds = tc[1] - tc[3], ts[1] - ts[3]
    uc = jnp.concatenate([ac + cc, bc - ds, ac - cc, bc + ds], axis=0).astype(BF16)
    us = jnp.concatenate([as_ + cs, bs + dc, as_ - cs, bs - dc], axis=0).astype(BF16)
    fg = d // N_FGROUPS
    chan = chan_ref[...]
    f = []
    for g in range(N_FGROUPS):
        lhs = jnp.concatenate([uc[:, g * fg:(g + 1) * fg], us[:, g * fg:(g + 1) * fg]], axis=1)
        f.append(jnp.dot(lhs, chan, preferred_element_type=F32))
    f = jnp.concatenate(f, axis=1).astype(BF16)
    y = jnp.dot(f, w_ref[...], preferred_element_type=F32)
    o_ref[0] = x_ref[0] + y.reshape(RADIX, tk, d)


def _fourier_layer(x, g, w_out):
    b, s, d = x.shape
    n = s // RADIX
    tk = min(128, n)
    tm = min(1024, s)
    xr = _norm_cast(x.reshape(b * s, d), g, tm).reshape(b, n, RADIX * d)
    dft, tw, chan = _fourier_tables(s, d, tk)
    dft = jnp.asarray(dft).astype(BF16)
    chan = jnp.asarray(chan).astype(BF16)
    x4 = x.reshape(b, RADIX, n, d)
    fg = d // N_FGROUPS
    out = pl.pallas_call(
        functools.partial(_fourier_kernel, tk=tk, d=d),
        grid=(b, n // tk),
        in_specs=[
            pl.BlockSpec((1, 2 * tk, n), lambda i, j: (j, 0, 0)),
            pl.BlockSpec((1, n, RADIX * d), lambda i, j: (i, 0, 0)),
            pl.BlockSpec((tk, 8), lambda i, j: (j, 0)),
            _resident((2 * fg, fg)),
            _resident((d, d)),
            pl.BlockSpec((1, RADIX, tk, d), lambda i, j: (i, 0, j, 0)),
        ],
        out_specs=pl.BlockSpec((1, RADIX, tk, d), lambda i, j: (i, 0, j, 0)),
        out_shape=jax.ShapeDtypeStruct((b, RADIX, n, d), F32),
        compiler_params=_params(2),
        name="fourier",
    )(dft, xr, jnp.asarray(tw), chan, w_out.astype(BF16), x4)
    return out.reshape(b, s, d)


def _head_lane_order():
    quarter = AXIS_DIM // 2
    blocks = np.arange(HEAD_DIM).reshape(4, quarter)
    return np.concatenate([blocks[0], blocks[2], blocks[1], blocks[3]])


def _rope_tables(s):
    pos = np.arange(s)
    inv = ROPE_THETA ** (-np.arange(0, AXIS_DIM, 2, dtype=np.float64) / AXIS_DIM)
    ang = np.concatenate([(pos // GRID_W)[:, None] * inv[None, :],
                          (pos % GRID_W)[:, None] * inv[None, :]], axis=1)
    cos = np.concatenate([np.cos(ang), np.cos(ang)], axis=1)
    sin = np.concatenate([-np.sin(ang), np.sin(ang)], axis=1)
    return cos.astype(np.float32), sin.astype(np.float32)


def _rope(x, cos, sin):
    return x * cos + pltpu.roll(x, HEAD_DIM // 2, 1) * sin


def _store_heads(raw, gq_ref, gk_ref, cos_ref, sin_ref, q_ref, k_ref, v_ref):
    cos, sin = cos_ref[...], sin_ref[...]
    for h in range(N_HEADS):
        qh = _rms(raw[:, h * HEAD_DIM:(h + 1) * HEAD_DIM], gq_ref[...])
        q_ref[0, h] = _rope(qh, cos, sin).astype(BF16)
    lane = lax.broadcasted_iota(jnp.int32, cos.shape, 1)
    ones_col = jnp.where(lane == 0, 1.0, 0.0).astype(BF16)
    for h in range(N_KV_HEADS):
        off = (N_HEADS + h) * HEAD_DIM
        kh = _rms(raw[:, off:off + HEAD_DIM], gk_ref[...])
        k_ref[0, h] = _rope(kh, cos, sin).astype(BF16)
        off = (N_HEADS + N_KV_HEADS + h) * HEAD_DIM
        v_ref[0, h, :, :HEAD_DIM] = raw[:, off:off + HEAD_DIM].astype(BF16)
        v_ref[0, h, :, HEAD_DIM:] = ones_col


def _mlp_block(x, g_ref, wup_ref, wdown_ref, ff_chunk):
    xn = _rms(x, g_ref[...]).astype(BF16)
    acc = x
    for c in range(wup_ref.shape[1] // ff_chunk):
        h = jnp.dot(xn, wup_ref[:, c * ff_chunk:(c + 1) * ff_chunk], preferred_element_type=F32)
        h = jnp.maximum(h, 0.0)
        h = (h * h).astype(BF16)
        acc = acc + jnp.dot(h, wdown_ref[c * ff_chunk:(c + 1) * ff_chunk, :],
                            preferred_element_type=F32)
    return acc


def _mlp_kernel(*refs, has_proj, has_final, ff_chunk):
    refs = list(refs)
    x_ref = refs.pop(0)
    if has_proj:
        a_ref, wo_ref = refs.pop(0), refs.pop(0)
    g_ref, wup_ref, wdown_ref = refs.pop(0), refs.pop(0), refs.pop(0)
    if has_final:
        gf_ref = refs.pop(0)
    o_ref = refs.pop(0)

    x = x_ref[...]
    if has_proj:
        x = x + jnp.dot(a_ref[...], wo_ref[...], preferred_element_type=F32)
    acc = _mlp_block(x, g_ref, wup_ref, wdown_ref, ff_chunk)
    if has_final:
        acc = _rms(acc, gf_ref[...])
    o_ref[...] = acc


def _mlp_layer(x2d, g, w_up, w_down, attn=None, w_o=None, g_final=None):
    t, d = x2d.shape
    ff = w_up.shape[1]
    tm = min(512, t)
    has_proj = attn is not None
    has_final = g_final is not None
    row = pl.BlockSpec((tm, d), lambda i: (i, 0))
    args, specs = [x2d], [row]
    if has_proj:
        args += [attn, w_o.astype(BF16)]
        specs += [row, _resident((d, d))]
    args += [g.reshape(1, d), w_up.astype(BF16), w_down.astype(BF16)]
    specs += [_resident((1, d)), _resident((d, ff)), _resident((ff, d))]
    if has_final:
        args.append(g_final.reshape(1, d))
        specs.append(_resident((1, d)))
    return pl.pallas_call(
        functools.partial(_mlp_kernel, has_proj=has_proj, has_final=has_final,
                          ff_chunk=min(1024, ff)),
        grid=(t // tm,),
        in_specs=specs,
        out_specs=row,
        out_shape=jax.ShapeDtypeStruct((t, d), F32),
        compiler_params=_params(1),
        name="mlp_proj" if has_proj else "mlp",
    )(*args)


def _mlp_qkv_kernel(x_ref, g_ref, wup_ref, wdown_ref, ga_ref, wqkv_ref, gq_ref, gk_ref,
                    cos_ref, sin_ref, o_ref, q_ref, k_ref, v_ref, raw_ref, *, ff_chunk):
    i = pl.program_id(0)
    slot = i % 2

    @pl.when(i == 0)
    def _():
        raw_ref[1] = jnp.zeros(raw_ref.shape[1:], F32)

    _store_heads(raw_ref.at[1 - slot], gq_ref, gk_ref, cos_ref, sin_ref, q_ref, k_ref, v_ref)
    acc = _mlp_block(x_ref[...], g_ref, wup_ref, wdown_ref, ff_chunk)
    o_ref[...] = acc
    xn = _rms(acc, ga_ref[...]).astype(BF16)
    raw_ref[slot] = jnp.dot(xn, wqkv_ref[...], preferred_element_type=F32)


def _mlp_qkv_layer(x, g, w_up, w_down, g_attn, w_qkv, gq, gk):
    b, s, d = x.shape
    ff = w_up.shape[1]
    tm = min(512, s)
    per_b = s // tm
    nt = b * per_b
    cos, sin = _rope_tables(s)
    qkv_dim = w_qkv.shape[1]
    n_qk = (N_HEADS + N_KV_HEADS) * HEAD_DIM
    order = _head_lane_order()
    heads = np.arange(N_HEADS + N_KV_HEADS)[:, None] * HEAD_DIM
    cols = np.concatenate([(heads + order[None, :]).ravel(), np.arange(n_qk, qkv_dim)])
    w = w_qkv[:, cols].astype(BF16)
    q_scale = HEAD_DIM ** -0.5 * np.log2(np.e)
    cur = lambda i: jnp.minimum(i, nt - 1)
    prev = lambda i: jnp.maximum(i - 1, 0)
    row = pl.BlockSpec((tm, d), lambda i: (cur(i), 0))
    head = lambda nh, width: pl.BlockSpec(
        (1, nh, tm, width), lambda i: (prev(i) // per_b, 0, prev(i) % per_b, 0))
    tab = pl.BlockSpec((tm, HEAD_DIM), lambda i: (prev(i) % per_b, 0))
    x2, q, k, v = pl.pallas_call(
        functools.partial(_mlp_qkv_kernel, ff_chunk=min(1024, ff)),
        grid=(nt + 1,),
        in_specs=[
            row, _resident((1, d)), _resident((d, ff)), _resident((ff, d)),
            _resident((1, d)), _resident((d, qkv_dim)),
            _resident((1, HEAD_DIM)), _resident((1, HEAD_DIM)), tab, tab,
        ],
        out_specs=[row, head(N_HEADS, HEAD_DIM), head(N_KV_HEADS, HEAD_DIM),
                   head(N_KV_HEADS, 2 * HEAD_DIM)],
        out_shape=[
            jax.ShapeDtypeStruct((b * s, d), F32),
            jax.ShapeDtypeStruct((b, N_HEADS, s, HEAD_DIM), BF16),
            jax.ShapeDtypeStruct((b, N_KV_HEADS, s, HEAD_DIM), BF16),
            jax.ShapeDtypeStruct((b, N_KV_HEADS, s, 2 * HEAD_DIM), BF16),
        ],
        scratch_shapes=[pltpu.VMEM((2, tm, qkv_dim), F32)],
        compiler_params=_params(1),
        name="mlp_qkv",
    )(x.reshape(b * s, d), g.reshape(1, d), w_up.astype(BF16), w_down.astype(BF16),
      g_attn.reshape(1, d), w, (gq[order] * q_scale).reshape(1, HEAD_DIM),
      gk[order].reshape(1, HEAD_DIM), jnp.asarray(cos), jnp.asarray(sin))
    return x2.reshape(b, s, d), q, k, v


def _attn_kernel(q_ref, k_ref, v_ref, o_ref, s0, s1, p0, p1, *, tq, nq):
    s_buf, p_buf = (s0, s1), (p0, p1)
    seq = k_ref.shape[2]

    def scores(row, h, slot):
        q = q_ref[0, h, pl.ds(row, tq), :]
        s_buf[slot][:, :seq] = lax.dot_general(q, k_ref[0, 0], (((1,), (1,)), ((), ())),
                                               preferred_element_type=F32)

    def softmax(slot):
        sb, pb = s_buf[slot], p_buf[slot]
        m = sb[:, 0:LANES]
        for c in range(1, seq // LANES):
            m = jnp.maximum(m, sb[:, c * LANES:(c + 1) * LANES])
        mb = jnp.broadcast_to(jnp.max(m, axis=-1, keepdims=True), (tq, LANES))
        for c in range(seq // LANES):
            p = jnp.exp2(sb[:, c * LANES:(c + 1) * LANES] - mb)
            pb[:, c * LANES:(c + 1) * LANES] = p.astype(BF16)

    def weighted_values(row, h, slot):
        o = jnp.dot(p_buf[slot][:, :seq], v_ref[0, 0], preferred_element_type=F32)
        o = o[:, :HEAD_DIM] / o[:, HEAD_DIM:HEAD_DIM + 1]
        o_ref[0, pl.ds(row, tq), h * HEAD_DIM:(h + 1) * HEAD_DIM] = o.astype(BF16)

    p1[...] = jnp.ones_like(p1)
    scores(0, 0, 0)

    def tile_ticks(qi):
        row = pl.multiple_of(qi * tq, tq)
        row_next = pl.multiple_of(jnp.minimum(qi + 1, nq - 1) * tq, tq)
        row_prev = pl.multiple_of(jnp.maximum(qi - 1, 0) * tq, tq)
        for h in range(KV_GROUP):
            slot = h % 2
            if h + 1 < KV_GROUP:
                scores(row, h + 1, 1 - slot)
            else:
                scores(row_next, 0, 1 - slot)
            softmax(slot)
            if h > 0:
                weighted_values(row, h - 1, 1 - slot)
            else:
                weighted_values(row_prev, KV_GROUP - 1, 1 - slot)

    def body(qj, carry):
        for u in range(TILES_PER_TRIP):
            tile_ticks(qj * TILES_PER_TRIP + u)
        return carry

    lax.fori_loop(0, nq // TILES_PER_TRIP, body, 0)
    weighted_values((nq - 1) * tq, KV_GROUP - 1, 1)


def _attention(q, k, v):
    b, _, s, _ = q.shape
    tq = min(256, s)
    nq = s // tq
    assert s % tq == 0 and nq % TILES_PER_TRIP == 0
    whole =lambda nh, width: pl.BlockSpec((1, nh, s, width), lambda i, g: (i, g, 0, 0))
    return pl.pallas_call(
        functools.partial(_attn_kernel, tq=tq, nq=nq),
        grid=(b, N_KV_HEADS),
        in_specs=[whole(KV_GROUP, HEAD_DIM), whole(1, HEAD_DIM), whole(1, 2 * HEAD_DIM)],
        out_specs=pl.BlockSpec((1, s, KV_GROUP * HEAD_DIM), lambda i, g: (i, 0, g)),
        out_shape=jax.ShapeDtypeStruct((b, s, N_HEADS * HEAD_DIM), BF16),
        scratch_shapes=[pltpu.VMEM((tq, s + LANES), F32), pltpu.VMEM((tq, s + LANES), F32),
                        pltpu.VMEM((tq, s + LANES), BF16), pltpu.VMEM((tq, s + LANES), BF16)],
        compiler_params=_params(2),
        name="attention",
    )(q, k, v)


def _trunk(x, fourier_norm, fourier_w_out, attn_norm, attn_w_qkv, attn_q_norm, attn_k_norm,
           attn_w_o, mlp_norm, mlp_w_up, mlp_w_down, final_norm):
    b, s, d = x.shape
    depth = mlp_norm.shape[0]
    qkv = None
    for i in range(depth):
        j = i // 2
        g_final = final_norm if i == depth - 1 else None
        if i % 2 == 0:
            x = _fourier_layer(x, fourier_norm[j], fourier_w_out[j])
            if i + 1 < depth:
                x, *qkv = _mlp_qkv_layer(x, mlp_norm[i], mlp_w_up[i], mlp_w_down[i], attn_norm[j],
                                         attn_w_qkv[j], attn_q_norm[j], attn_k_norm[j])
            else:
                x = _mlp_layer(x.reshape(b * s, d), mlp_norm[i], mlp_w_up[i], mlp_w_down[i],
                               g_final=g_final).reshape(b, s, d)
        else:
            a = _attention(*qkv)
            x = _mlp_layer(x.reshape(b * s, d), mlp_norm[i], mlp_w_up[i], mlp_w_down[i],
                           attn=a.reshape(b * s, d), w_o=attn_w_o[j],
                           g_final=g_final).reshape(b, s, d)
    return x


def kernel(x_prompt, x_sample, fourier_norm, fourier_w_out, attn_norm, attn_w_qkv, attn_q_norm,
           attn_k_norm, attn_w_o, mlp_norm, mlp_w_up, mlp_w_down, final_norm):
    weights = (fourier_norm, fourier_w_out, attn_norm, attn_w_qkv, attn_q_norm, attn_k_norm,
               attn_w_o, mlp_norm, mlp_w_up, mlp_w_down, final_norm)
    return (_trunk(x_prompt, *weights), _trunk(x_sample, *weights))
```

```python
import functools

import numpy as np
import jax
import jax.numpy as jnp
from jax import lax
from jax.experimental import pallas as pl
from jax.experimental.pallas import tpu as pltpu

F32 = jnp.float32
BF16 = jnp.bfloat16

N_FGROUPS = 8
HEAD_DIM = 128
N_HEADS = 8
N_KV_HEADS = 2
KV_GROUP = N_HEADS // N_KV_HEADS
AXIS_DIM = HEAD_DIM // 2
ROPE_THETA = 10000.0
GRID_W = 64
EPS = 1e-6

V7X_VMEM_BYTES = 64 * 1024 * 1024
VMEM_LIMIT_BYTES = V7X_VMEM_BYTES - 8 * 1024 * 1024
LANES = 128

RADIX = 8
FOURIER_ROWS = 512
TILES_PER_TRIP = 4


def _params(n_grid_dims):
    return pltpu.CompilerParams(
        dimension_semantics=("arbitrary",) * n_grid_dims,
        vmem_limit_bytes=VMEM_LIMIT_BYTES,
    )


def _rms(x, g):
    return x * lax.rsqrt(jnp.mean(x * x, axis=-1, keepdims=True) + EPS) * g


def _resident(shape):
    nd = len(shape)
    return pl.BlockSpec(shape, lambda *_: (0,) * nd, pipeline_mode=pl.Buffered(1))


def _norm_cast_kernel(x_ref, g_ref, o_ref, cols_ref, *, d):
    rows = o_ref.shape[0]
    xn = _rms(x_ref[...], g_ref[...])
    for c in range(d // LANES):
        cols_ref[c] = xn[:, c * LANES:(c + 1) * LANES]
    for j in range(RADIX):
        for c in range(d // LANES):
            o_ref[:, j * d + c * LANES:j * d + (c + 1) * LANES] = (
                cols_ref[c, pl.ds(j, rows, stride=RADIX), :].astype(BF16))


def _norm_cast(x2d, g, tm):
    t, d = x2d.shape
    return pl.pallas_call(
        functools.partial(_norm_cast_kernel, d=d),
        grid=(t // tm,),
        in_specs=[pl.BlockSpec((tm, d), lambda i: (i, 0)), _resident((1, d))],
        out_specs=pl.BlockSpec((tm // RADIX, RADIX * d), lambda i: (i, 0)),
        out_shape=jax.ShapeDtypeStruct((t // RADIX, RADIX * d), BF16),
        scratch_shapes=[pltpu.VMEM((d // LANES, tm, LANES), F32)],
        compiler_params=_params(1),
        name="norm_cast",
    )(x2d, g.reshape(1, d))


def _fourier_tables(s, d, tk):
    n = s // RADIX
    k = np.arange(n)
    ang = 2.0 * np.pi * ((k[:, None] * k[None, :]) % n) / n
    c = np.cos(ang).reshape(n // tk, tk, n)
    sn = np.sin(ang).reshape(n // tk, tk, n)
    dft = np.concatenate([c, -sn], axis=1).astype(np.float32)
    tw = np.zeros((n, 2 * RADIX), np.float32)
    for j in range(1, RADIX):
        th = 2.0 * np.pi * ((k * j) % s) / s
        tw[:, 2 * (j - 1)] = np.cos(th)
        tw[:, 2 * (j - 1) + 1] = -np.sin(th)
    fg = d // N_FGROUPS
    cc = np.arange(fg)
    cang = 2.0 * np.pi * ((cc[:, None] * cc[None, :]) % fg) / fg
    scale = 1.0 / np.sqrt(float(s) * fg)
    chan = np.concatenate([np.cos(cang), np.sin(cang)], axis=0) * scale
    return dft, tw, chan.astype(np.float32)


def _times_root(z, m, r):
    re, im = z
    m %= r
    if m == 0:
        return re, im
    if 4 * m == r:
        return im, -re
    if 2 * m == r:
        return -re, -im
    if 4 * m == 3 * r:
        return -im, re
    wr, wi = float(np.cos(2.0 * np.pi * m / r)), float(-np.sin(2.0 * np.pi * m / r))
    if 8 * m % r == 0:
        a = abs(wr)
        sr, si = np.sign(wr), np.sign(wi)
        return (sr * re - si * im) * a, (si * re + sr * im) * a
    return re * wr - im * wi, re * wi + im * wr


def _small_dft(zs):
    r = len(zs)
    if r == 1:
        return zs
    even, odd = _small_dft(zs[0::2]), _small_dft(zs[1::2])
    out = [None] * r
    for q in range(r // 2):
        tr, ti = _times_root(odd[q], q, r)
        out[q] = (even[q][0] + tr, even[q][1] + ti)
        out[q + r // 2] = (even[q][0] - tr, even[q][1] - ti)
    return out


def _fourier_kernel(dft_ref, xr_ref, tw_ref, chan_ref, w_ref, x_ref, o_ref, *, tk, d):
    e = jnp.dot(dft_ref[0], xr_ref[0], preferred_element_type=F32)
    tw = tw_ref[...]
    zs = [(e[:tk, 0:d], e[tk:, 0:d])]
    for j in range(1, RADIX):
        er, ei = e[:tk, j * d:(j + 1) * d], e[tk:, j * d:(j + 1) * d]
        tr = tw[:, 2 * (j - 1):2 * (j - 1) + 1]
        ti = tw[:, 2 * (j - 1) + 1:2 * (j - 1) + 2]
        zs.append((er * tr - ei * ti, er * ti + ei * tr))
    us = _small_dft(zs)
    ur = jnp.concatenate([u[0] for u in us], axis=0).astype(BF16)
    ui = jnp.concatenate([u[1] for u in us], axis=0).astype(BF16)
    fg = d // N_FGROUPS
    chan = chan_ref[...]
    f = []
    for g in range(N_FGROUPS):
        lhs = jnp.concatenate([ur[:, g * fg:(g + 1) * fg], ui[:, g * fg:(g + 1) * fg]], axis=1)
        f.append(jnp.dot(lhs, chan, preferred_element_type=F32))
    f = jnp.concatenate(f, axis=1).astype(BF16)
    y = jnp.dot(f, w_ref[...], preferred_element_type=F32)
    o_ref[0] = x_ref[0] + y.reshape(RADIX, tk, d)


def _fourier_layer(x, g, w_out):
    b, s, d = x.shape
    n = s // RADIX
    tk = min(FOURIER_ROWS // RADIX, n)
    tm = min(1024, s)
    xr = _norm_cast(x.reshape(b * s, d), g, tm).reshape(b, n, RADIX * d)
    dft, tw, chan = _fourier_tables(s, d, tk)
    dft = jnp.asarray(dft).astype(BF16)
    chan = jnp.asarray(chan).astype(BF16)
    x4 = x.reshape(b, RADIX, n, d)
    fg = d // N_FGROUPS
    out = pl.pallas_call(
        functools.partial(_fourier_kernel, tk=tk, d=d),
        grid=(b, n // tk),
        in_specs=[
            pl.BlockSpec((1, 2 * tk, n), lambda i, j: (j, 0, 0)),
            pl.BlockSpec((1, n, RADIX * d), lambda i, j: (i, 0, 0)),
            pl.BlockSpec((tk, 2 * RADIX), lambda i, j: (j, 0)),
            _resident((2 * fg, fg)),
            _resident((d, d)),
            pl.BlockSpec((1, RADIX, tk, d), lambda i, j: (i, 0, j, 0)),
        ],
        out_specs=pl.BlockSpec((1, RADIX, tk, d), lambda i, j: (i, 0, j, 0)),
        out_shape=jax.ShapeDtypeStruct((b, RADIX, n, d), F32),
        compiler_params=_params(2),
        name="fourier",
    )(dft, xr, jnp.asarray(tw), chan, w_out.astype(BF16), x4)
    return out.reshape(b, s, d)


def _head_lane_order():
    quarter = AXIS_DIM // 2
    blocks = np.arange(HEAD_DIM).reshape(4, quarter)
    return np.concatenate([blocks[0], blocks[2], blocks[1], blocks[3]])


def _rope_tables(s):
    pos = np.arange(s)
    inv = ROPE_THETA ** (-np.arange(0, AXIS_DIM, 2, dtype=np.float64) / AXIS_DIM)
    ang = np.concatenate([(pos // GRID_W)[:, None] * inv[None, :],
                          (pos % GRID_W)[:, None] * inv[None, :]], axis=1)
    cos = np.concatenate([np.cos(ang), np.cos(ang)], axis=1)
    sin = np.concatenate([-np.sin(ang), np.sin(ang)], axis=1)
    return cos.astype(np.float32), sin.astype(np.float32)


def _rope(x, cos, sin):
    return x * cos + pltpu.roll(x, HEAD_DIM // 2, 1) * sin


def _store_heads(raw, gq_ref, gk_ref, cos_ref, sin_ref, q_ref, k_ref, v_ref):
    cos, sin = cos_ref[...], sin_ref[...]
    for h in range(N_HEADS):
        qh = _rms(raw[:, h * HEAD_DIM:(h + 1) * HEAD_DIM], gq_ref[...])
        q_ref[0, h] = _rope(qh, cos, sin).astype(BF16)
    lane = lax.broadcasted_iota(jnp.int32, cos.shape, 1)
    ones_col = jnp.where(lane == 0, 1.0, 0.0).astype(BF16)
    for h in range(N_KV_HEADS):
        off = (N_HEADS + h) * HEAD_DIM
        kh = _rms(raw[:, off:off + HEAD_DIM], gk_ref[...])
        k_ref[0, h] = _rope(kh, cos, sin).astype(BF16)
        off = (N_HEADS + N_KV_HEADS + h) * HEAD_DIM
        v_ref[0, h, :, :HEAD_DIM] = raw[:, off:off + HEAD_DIM].astype(BF16)
        v_ref[0, h, :, HEAD_DIM:] = ones_col


def _mlp_block(x, g_ref, wup_ref, wdown_ref, ff_chunk):
    xn = _rms(x, g_ref[...]).astype(BF16)
    acc = x
    for c in range(wup_ref.shape[1] // ff_chunk):
        h = jnp.dot(xn, wup_ref[:, c * ff_chunk:(c + 1) * ff_chunk], preferred_element_type=F32)
        h = jnp.maximum(h, 0.0)
        h = (h * h).astype(BF16)
        acc = acc + jnp.dot(h, wdown_ref[c * ff_chunk:(c + 1) * ff_chunk, :],
                            preferred_element_type=F32)
    return acc


def _mlp_kernel(*refs, has_proj, has_final, ff_chunk):
    refs = list(refs)
    x_ref = refs.pop(0)
    if has_proj:
        a_ref, wo_ref = refs.pop(0), refs.pop(0)
    g_ref, wup_ref, wdown_ref = refs.pop(0), refs.pop(0), refs.pop(0)
    if has_final:
        gf_ref = refs.pop(0)
    o_ref = refs.pop(0)

    x = x_ref[...]
    if has_proj:
        x = x + jnp.dot(a_ref[...], wo_ref[...], preferred_element_type=F32)
    acc = _mlp_block(x, g_ref, wup_ref, wdown_ref, ff_chunk)
    if has_final:
        acc = _rms(acc, gf_ref[...])
    o_ref[...] = acc


def _mlp_layer(x2d, g, w_up, w_down, attn=None, w_o=None, g_final=None):
    t, d = x2d.shape
    ff = w_up.shape[1]
    tm = min(512, t)
    has_proj = attn is not None
    has_final = g_final is not None
    row = pl.BlockSpec((tm, d), lambda i: (i, 0))
    args, specs = [x2d], [row]
    if has_proj:
        args += [attn, w_o.astype(BF16)]
        specs += [row, _resident((d, d))]
    args += [g.reshape(1, d), w_up.astype(BF16), w_down.astype(BF16)]
    specs += [_resident((1, d)), _resident((d, ff)), _resident((ff, d))]
    if has_final:
        args.append(g_final.reshape(1, d))
        specs.append(_resident((1, d)))
    return pl.pallas_call(
        functools.partial(_mlp_kernel, has_proj=has_proj, has_final=has_final,
                          ff_chunk=min(1024, ff)),
        grid=(t // tm,),
        in_specs=specs,
        out_specs=row,
        out_shape=jax.ShapeDtypeStruct((t, d), F32),
        compiler_params=_params(1),
        name="mlp_proj" if has_proj else "mlp",
    )(*args)


def _mlp_qkv_kernel(x_ref, g_ref, wup_ref, wdown_ref, ga_ref, wqkv_ref, gq_ref, gk_ref,
                    cos_ref, sin_ref, o_ref, q_ref, k_ref, v_ref, raw_ref, *, ff_chunk):
    i = pl.program_id(0)
    slot = i % 2

    @pl.when(i == 0)
    def _():
        raw_ref[1] = jnp.zeros(raw_ref.shape[1:], F32)

    _store_heads(raw_ref.at[1 - slot], gq_ref, gk_ref, cos_ref, sin_ref, q_ref, k_ref, v_ref)
    acc = _mlp_block(x_ref[...], g_ref, wup_ref, wdown_ref, ff_chunk)
    o_ref[...] = acc
    xn = _rms(acc, ga_ref[...]).astype(BF16)
    raw_ref[slot] = jnp.dot(xn, wqkv_ref[...], preferred_element_type=F32)


def _mlp_qkv_layer(x, g, w_up, w_down, g_attn, w_qkv, gq, gk):
    b, s, d = x.shape
    ff = w_up.shape[1]
    tm = min(512, s)
    per_b = s // tm
    nt = b * per_b
    cos, sin = _rope_tables(s)
    qkv_dim = w_qkv.shape[1]
    n_qk = (N_HEADS + N_KV_HEADS) * HEAD_DIM
    order = _head_lane_order()
    heads = np.arange(N_HEADS + N_KV_HEADS)[:, None] * HEAD_DIM
    cols = np.concatenate([(heads + order[None, :]).ravel(), np.arange(n_qk, qkv_dim)])
    w = w_qkv[:, cols].astype(BF16)
    q_scale = HEAD_DIM ** -0.5 * np.log2(np.e)
    cur = lambda i: jnp.minimum(i, nt - 1)
    prev = lambda i: jnp.maximum(i - 1, 0)
    row = pl.BlockSpec((tm, d), lambda i: (cur(i), 0))
    head = lambda nh, width: pl.BlockSpec(
        (1, nh, tm, width), lambda i: (prev(i) // per_b, 0, prev(i) % per_b, 0))
    tab = pl.BlockSpec((tm, HEAD_DIM), lambda i: (prev(i) % per_b, 0))
    x2, q, k, v = pl.pallas_call(
        functools.partial(_mlp_qkv_kernel, ff_chunk=min(1024, ff)),
        grid=(nt + 1,),
        in_specs=[
            row, _resident((1, d)), _resident((d, ff)), _resident((ff, d)),
            _resident((1, d)), _resident((d, qkv_dim)),
            _resident((1, HEAD_DIM)), _resident((1, HEAD_DIM)), tab, tab,
        ],
        out_specs=[row, head(N_HEADS, HEAD_DIM), head(N_KV_HEADS, HEAD_DIM),
                   head(N_KV_HEADS, 2 * HEAD_DIM)],
        out_shape=[
            jax.ShapeDtypeStruct((b * s, d), F32),
            jax.ShapeDtypeStruct((b, N_HEADS, s, HEAD_DIM), BF16),
            jax.ShapeDtypeStruct((b, N_KV_HEADS, s, HEAD_DIM), BF16),
            jax.ShapeDtypeStruct((b, N_KV_HEADS, s, 2 * HEAD_DIM), BF16),
        ],
        scratch_shapes=[pltpu.VMEM((2, tm, qkv_dim), F32)],
        compiler_params=_params(1),
        name="mlp_qkv",
    )(x.reshape(b * s, d), g.reshape(1, d), w_up.astype(BF16), w_down.astype(BF16),
      g_attn.reshape(1, d), w, (gq[order] * q_scale).reshape(1, HEAD_DIM),
      gk[order].reshape(1, HEAD_DIM), jnp.asarray(cos), jnp.asarray(sin))
    return x2.reshape(b, s, d), q, k, v


def _attn_kernel(q_ref, k_ref, v_ref, o_ref, s0, s1, p0, p1, *, tq, nq):
    s_buf, p_buf = (s0, s1), (p0, p1)
    seq = k_ref.shape[2]

    def scores(row, h, slot):
        q = q_ref[0, h, pl.ds(row, tq), :]
        s_buf[slot][:, :seq] = lax.dot_general(q, k_ref[0, 0], (((1,), (1,)), ((), ())),
                                               preferred_element_type=F32)

    def softmax(slot):
        sb, pb = s_buf[slot], p_buf[slot]
        m = sb[:, 0:LANES]
        for c in range(1, seq // LANES):
            m = jnp.maximum(m, sb[:, c * LANES:(c + 1) * LANES])
        mb = jnp.broadcast_to(jnp.max(m, axis=-1, keepdims=True), (tq, LANES))
        for c in range(seq // LANES):
            p = jnp.exp2(sb[:, c * LANES:(c + 1) * LANES] - mb)
            pb[:, c * LANES:(c + 1) * LANES] = p.astype(BF16)

    def weighted_values(row, h, slot):
        o = jnp.dot(p_buf[slot][:, :seq], v_ref[0, 0], preferred_element_type=F32)
        o = o[:, :HEAD_DIM] / o[:, HEAD_DIM:HEAD_DIM + 1]
        o_ref[0, pl.ds(row, tq), h * HEAD_DIM:(h + 1) * HEAD_DIM] = o.astype(BF16)

    p1[...] = jnp.ones_like(p1)
    scores(0, 0, 0)

    def tile_ticks(qi):
        row = pl.multiple_of(qi * tq, tq)
        row_next = pl.multiple_of(jnp.minimum(qi + 1, nq - 1) * tq, tq)
        row_prev = pl.multiple_of(jnp.maximum(qi - 1, 0) * tq, tq)
        for h in range(KV_GROUP):
            slot = h % 2
            if h + 1 < KV_GROUP:
                scores(row, h + 1, 1 - slot)
            else:
                scores(row_next, 0, 1 - slot)
            softmax(slot)
            if h > 0:
                weighted_values(row, h - 1, 1 - slot)
            else:
                weighted_values(row_prev, KV_GROUP - 1, 1 - slot)

    def body(qj, carry):
        for u in range(TILES_PER_TRIP):
            tile_ticks(qj * TILES_PER_TRIP + u)
        return carry

    lax.fori_loop(0, nq // TILES_PER_TRIP, body, 0)
    weighted_values((nq - 1) * tq, KV_GROUP - 1, 1)


def _attention(q, k, v):
    b, _, s, _ = q.shape
    tq = min(128, s)
    nq = s // tq
    assert s % tq == 0 and nq % TILES_PER_TRIP == 0
    whole =lambda nh, width: pl.BlockSpec((1, nh, s, width), lambda i, g: (i, g, 0, 0))
    return pl.pallas_call(
        functools.partial(_attn_kernel, tq=tq, nq=nq),
        grid=(b, N_KV_HEADS),
        in_specs=[whole(KV_GROUP, HEAD_DIM), whole(1, HEAD_DIM), whole(1, 2 * HEAD_DIM)],
        out_specs=pl.BlockSpec((1, s, KV_GROUP * HEAD_DIM), lambda i, g: (i, 0, g)),
        out_shape=jax.ShapeDtypeStruct((b, s, N_HEADS * HEAD_DIM), BF16),
        scratch_shapes=[pltpu.VMEM((tq, s + LANES), F32), pltpu.VMEM((tq, s + LANES), F32),
                        pltpu.VMEM((tq, s + LANES), BF16), pltpu.VMEM((tq, s + LANES), BF16)],
        compiler_params=_params(2),
        name="attention",
    )(q, k, v)


def _trunk(x, fourier_norm, fourier_w_out, attn_norm, attn_w_qkv, attn_q_norm, attn_k_norm,
           attn_w_o, mlp_norm, mlp_w_up, mlp_w_down, final_norm):
    b, s, d = x.shape
    depth = mlp_norm.shape[0]
    qkv = None
    for i in range(depth):
        j = i // 2
        g_final = final_norm if i == depth - 1 else None
        if i % 2 == 0:
            x = _fourier_layer(x, fourier_norm[j], fourier_w_out[j])
            if i + 1 < depth:
                x, *qkv = _mlp_qkv_layer(x, mlp_norm[i], mlp_w_up[i], mlp_w_down[i], attn_norm[j],
                                         attn_w_qkv[j], attn_q_norm[j], attn_k_norm[j])
            else:
                x = _mlp_layer(x.reshape(b * s, d), mlp_norm[i], mlp_w_up[i], mlp_w_down[i],
                               g_final=g_final).reshape(b, s, d)
        else:
            a = _attention(*qkv)
            x = _mlp_layer(x.reshape(b * s, d), mlp_norm[i], mlp_w_up[i], mlp_w_down[i],
                           attn=a.reshape(b * s, d), w_o=attn_w_o[j],
                           g_final=g_final).reshape(b, s, d)
    return x


def kernel(x_prompt, x_sample, fourier_norm, fourier_w_out, attn_norm, attn_w_qkv, attn_q_norm,
           attn_k_norm, attn_w_o, mlp_norm, mlp_w_up, mlp_w_down, final_norm):
    weights = (fourier_norm, fourier_w_out, attn_norm, attn_w_qkv, attn_q_norm, attn_k_norm,
               attn_w_o, mlp_norm, mlp_w_up, mlp_w_down, final_norm)
    return (_trunk(x_prompt, *weights), _trunk(x_sample, *weights))
```

```python
import functools

import numpy as np
import jax
import jax.numpy as jnp
from jax import lax
from jax.experimental import pallas as pl
from jax.experimental.pallas import tpu as pltpu

F32 = jnp.float32
BF16 = jnp.bfloat16

N_FGROUPS = 8
HEAD_DIM = 128
N_HEADS = 8
N_KV_HEADS = 2
KV_GROUP = N_HEADS // N_KV_HEADS
AXIS_DIM = HEAD_DIM // 2
ROPE_THETA = 10000.0
GRID_W = 64
EPS = 1e-6

V7X_VMEM_BYTES = 64 * 1024 * 1024
VMEM_LIMIT_BYTES = V7X_VMEM_BYTES - 8 * 1024 * 1024
LANES = 128

RADIX = 8
FOURIER_ROWS = 512
TILES_PER_TRIP = 4


def _params(n_grid_dims):
    return pltpu.CompilerParams(
        dimension_semantics=("arbitrary",) * n_grid_dims,
        vmem_limit_bytes=VMEM_LIMIT_BYTES,
    )


def _rms(x, g):
    return x * lax.rsqrt(jnp.mean(x * x, axis=-1, keepdims=True) + EPS) * g


def _resident(shape):
    nd = len(shape)
    return pl.BlockSpec(shape, lambda *_: (0,) * nd, pipeline_mode=pl.Buffered(1))


def _fourier_tables(s, d, tk):
    n = s // RADIX
    k = np.arange(n)
    ang = 2.0 * np.pi * ((k[:, None] * k[None, :]) % n) / n
    c = np.cos(ang).reshape(n // tk, tk, n)
    sn = np.sin(ang).reshape(n // tk, tk, n)
    dft = np.concatenate([c, -sn], axis=1).astype(np.float32)
    tw = np.zeros((n, 2 * RADIX), np.float32)
    for j in range(1, RADIX):
        th = 2.0 * np.pi * ((k * j) % s) / s
        tw[:, 2 * (j - 1)] = np.cos(th)
        tw[:, 2 * (j - 1) + 1] = -np.sin(th)
    fg = d // N_FGROUPS
    cc = np.arange(fg)
    cang = 2.0 * np.pi * ((cc[:, None] * cc[None, :]) % fg) / fg
    scale = 1.0 / np.sqrt(float(s) * fg)
    chan = np.concatenate([np.cos(cang), np.sin(cang)], axis=0) * scale
    return dft, tw, chan.astype(np.float32)


def _times_root(z, m, r):
    re, im = z
    m %= r
    if m == 0:
        return re, im
    if 4 * m == r:
        return im, -re
    if 2 * m == r:
        return -re, -im
    if 4 * m == 3 * r:
        return -im, re
    wr, wi = float(np.cos(2.0 * np.pi * m / r)), float(-np.sin(2.0 * np.pi * m / r))
    if 8 * m % r == 0:
        a = abs(wr)
        sr, si = np.sign(wr), np.sign(wi)
        return (sr * re - si * im) * a, (si * re + sr * im) * a
    return re * wr - im * wi, re * wi + im * wr


def _small_dft(zs):
    r = len(zs)
    if r == 1:
        return zs
    even, odd = _small_dft(zs[0::2]), _small_dft(zs[1::2])
    out = [None] * r
    for q in range(r // 2):
        tr, ti = _times_root(odd[q], q, r)
        out[q] = (even[q][0] + tr, even[q][1] + ti)
        out[q + r // 2] = (even[q][0] - tr, even[q][1] - ti)
    return out


def _normalise_rows(xt_ref, g_ref, cols_ref, xr_ref, j, d):
    rows = xt_ref.shape[1] // RADIX
    xn = _rms(xt_ref[0], g_ref[...])
    for c in range(d // LANES):
        cols_ref[c] = xn[:, c * LANES:(c + 1) * LANES]
    r0 = pl.multiple_of(j * rows, rows)
    for jr in range(RADIX):
        for c in range(d // LANES):
            xr_ref[pl.ds(r0, rows), jr * d + c * LANES:jr * d + (c + 1) * LANES] = (
                cols_ref[c, pl.ds(jr, rows, stride=RADIX), :].astype(BF16))


def _fourier_tile(dft_ref, xr_ref, tw_ref, chan_ref, w_ref, x_ref, o_ref, tk, d):
    e = jnp.dot(dft_ref[0], xr_ref[...], preferred_element_type=F32)
    tw = tw_ref[...]
    zs = [(e[:tk, 0:d], e[tk:, 0:d])]
    for j in range(1, RADIX):
        er, ei = e[:tk, j * d:(j + 1) * d], e[tk:, j * d:(j + 1) * d]
        tr = tw[:, 2 * (j - 1):2 * (j - 1) + 1]
        ti = tw[:, 2 * (j - 1) + 1:2 * (j - 1) + 2]
        zs.append((er * tr - ei * ti, er * ti + ei * tr))
    us = _small_dft(zs)
    ur = jnp.concatenate([u[0] for u in us], axis=0).astype(BF16)
    ui = jnp.concatenate([u[1] for u in us], axis=0).astype(BF16)
    fg = d // N_FGROUPS
    chan = chan_ref[...]
    f = []
    for g in range(N_FGROUPS):
        lhs = jnp.concatenate([ur[:, g * fg:(g + 1) * fg], ui[:, g * fg:(g + 1) * fg]], axis=1)
        f.append(jnp.dot(lhs, chan, preferred_element_type=F32))
    f = jnp.concatenate(f, axis=1).astype(BF16)
    y = jnp.dot(f, w_ref[...], preferred_element_type=F32)
    o_ref[0] = x_ref[0] + y.reshape(RADIX, tk, d)


def _fourier_kernel(dft_ref, xt_ref, g_ref, tw_ref, chan_ref, w_ref, x_ref, o_ref,
                    xr0, xr1, cols_ref, *, tk, d):
    i, j = pl.program_id(0), pl.program_id(1)

    @pl.when((i == 0) & (j == 0))
    def _():
        xr1[...] = jnp.zeros_like(xr1)

    def step(xr_fill, xr_mix):
        _fourier_tile(dft_ref, xr_mix, tw_ref, chan_ref, w_ref, x_ref, o_ref, tk, d)
        _normalise_rows(xt_ref, g_ref, cols_ref, xr_fill, j, d)

    @pl.when(i % 2 == 0)
    def _():
        step(xr0, xr1)

    @pl.when(i % 2 == 1)
    def _():
        step(xr1, xr0)


def _fourier_layer(x, g, w_out):
    b, s, d = x.shape
    n = s // RADIX
    tk = min(FOURIER_ROWS // RADIX, n)
    nk = n // tk
    tokens = s // nk
    dft, tw, chan = _fourier_tables(s, d, tk)
    dft = jnp.asarray(dft).astype(BF16)
    chan = jnp.asarray(chan).astype(BF16)
    x4 = x.reshape(b, RADIX, n, d)
    fg = d // N_FGROUPS
    fill = lambda i: jnp.minimum(i, b - 1)
    mix = lambda i: jnp.maximum(i - 1, 0)
    blocks = pl.BlockSpec((1, RADIX, tk, d), lambda i, j: (mix(i), 0, jnp.where(i == 0, 0, j), 0))
    out = pl.pallas_call(
        functools.partial(_fourier_kernel, tk=tk, d=d),
        grid=(b + 1, nk),
        in_specs=[
            pl.BlockSpec((1, 2 * tk, n), lambda i, j: (j, 0, 0)),
            pl.BlockSpec((1, tokens, d), lambda i, j: (fill(i), j, 0)),
            _resident((1, d)),
            pl.BlockSpec((tk, 2 * RADIX), lambda i, j: (j, 0)),
            _resident((2 * fg, fg)),
            _resident((d, d)),
            blocks,
        ],
        out_specs=blocks,
        out_shape=jax.ShapeDtypeStruct((b, RADIX, n, d), F32),
        scratch_shapes=[pltpu.VMEM((n, RADIX * d), BF16), pltpu.VMEM((n, RADIX * d), BF16),
                        pltpu.VMEM((d // LANES, tokens, LANES), F32)],
        compiler_params=_params(2),
        name="fourier",
    )(dft, x, g.reshape(1, d), jnp.asarray(tw), chan, w_out.astype(BF16), x4)
    return out.reshape(b, s, d)


def _head_lane_order():
    quarter = AXIS_DIM // 2
    blocks = np.arange(HEAD_DIM).reshape(4, quarter)
    return np.concatenate([blocks[0], blocks[2], blocks[1], blocks[3]])


def _rope_tables(s):
    pos = np.arange(s)
    inv = ROPE_THETA ** (-np.arange(0, AXIS_DIM, 2, dtype=np.float64) / AXIS_DIM)
    ang = np.concatenate([(pos // GRID_W)[:, None] * inv[None, :],
                          (pos % GRID_W)[:, None] * inv[None, :]], axis=1)
    cos = np.concatenate([np.cos(ang), np.cos(ang)], axis=1)
    sin = np.concatenate([-np.sin(ang), np.sin(ang)], axis=1)
    return cos.astype(np.float32), sin.astype(np.float32)


def _rope(x, cos, sin):
    return x * cos + pltpu.roll(x, HEAD_DIM // 2, 1) * sin


def _store_heads(raw, gq_ref, gk_ref, cos_ref, sin_ref, q_ref, k_ref, v_ref):
    cos, sin = cos_ref[...], sin_ref[...]
    for h in range(N_HEADS):
        qh = _rms(raw[:, h * HEAD_DIM:(h + 1) * HEAD_DIM], gq_ref[...])
        q_ref[0, h] = _rope(qh, cos, sin).astype(BF16)
    lane = lax.broadcasted_iota(jnp.int32, cos.shape, 1)
    ones_col = jnp.where(lane == 0, 1.0, 0.0).astype(BF16)
    for h in range(N_KV_HEADS):
        off = (N_HEADS + h) * HEAD_DIM
        kh = _rms(raw[:, off:off + HEAD_DIM], gk_ref[...])
        k_ref[0, h] = _rope(kh, cos, sin).astype(BF16)
        off = (N_HEADS + N_KV_HEADS + h) * HEAD_DIM
        v_ref[0, h, :, :HEAD_DIM] = raw[:, off:off + HEAD_DIM].astype(BF16)
        v_ref[0, h, :, HEAD_DIM:] = ones_col


def _mlp_block(x, g_ref, wup_ref, wdown_ref, ff_chunk):
    xn = _rms(x, g_ref[...]).astype(BF16)
    acc = x
    for c in range(wup_ref.shape[1] // ff_chunk):
        h = jnp.dot(xn, wup_ref[:, c * ff_chunk:(c + 1) * ff_chunk], preferred_element_type=F32)
        h = jnp.maximum(h, 0.0)
        h = (h * h).astype(BF16)
        acc = acc + jnp.dot(h, wdown_ref[c * ff_chunk:(c + 1) * ff_chunk, :],
                            preferred_element_type=F32)
    return acc


def _mlp_kernel(*refs, has_proj, has_final, ff_chunk):
    refs = list(refs)
    x_ref = refs.pop(0)
    if has_proj:
        a_ref, wo_ref = refs.pop(0), refs.pop(0)
    g_ref, wup_ref, wdown_ref = refs.pop(0), refs.pop(0), refs.pop(0)
    if has_final:
        gf_ref = refs.pop(0)
    o_ref = refs.pop(0)

    x = x_ref[...]
    if has_proj:
        x = x + jnp.dot(a_ref[...], wo_ref[...], preferred_element_type=F32)
    acc = _mlp_block(x, g_ref, wup_ref, wdown_ref, ff_chunk)
    if has_final:
        acc = _rms(acc, gf_ref[...])
    o_ref[...] = acc


def _mlp_layer(x2d, g, w_up, w_down, attn=None, w_o=None, g_final=None):
    t, d = x2d.shape
    ff = w_up.shape[1]
    tm = min(512, t)
    has_proj = attn is not None
    has_final = g_final is not None
    row = pl.BlockSpec((tm, d), lambda i: (i, 0))
    args, specs = [x2d], [row]
    if has_proj:
        args += [attn, w_o.astype(BF16)]
        specs += [row, _resident((d, d))]
    args += [g.reshape(1, d), w_up.astype(BF16), w_down.astype(BF16)]
    specs += [_resident((1, d)), _resident((d, ff)), _resident((ff, d))]
    if has_final:
        args.append(g_final.reshape(1, d))
        specs.append(_resident((1, d)))
    return pl.pallas_call(
        functools.partial(_mlp_kernel, has_proj=has_proj, has_final=has_final,
                          ff_chunk=min(1024, ff)),
        grid=(t // tm,),
        in_specs=specs,
        out_specs=row,
        out_shape=jax.ShapeDtypeStruct((t, d), F32),
        compiler_params=_params(1),
        name="mlp_proj" if has_proj else "mlp",
    )(*args)


def _mlp_qkv_kernel(x_ref, g_ref, wup_ref, wdown_ref, ga_ref, wqkv_ref, gq_ref, gk_ref,
                    cos_ref, sin_ref, o_ref, q_ref, k_ref, v_ref, raw_ref, *, ff_chunk):
    i = pl.program_id(0)
    slot = i % 2

    @pl.when(i == 0)
    def _():
        raw_ref[1] = jnp.zeros(raw_ref.shape[1:], F32)

    _store_heads(raw_ref.at[1 - slot], gq_ref, gk_ref, cos_ref, sin_ref, q_ref, k_ref, v_ref)
    acc = _mlp_block(x_ref[...], g_ref, wup_ref, wdown_ref, ff_chunk)
    o_ref[...] = acc
    xn = _rms(acc, ga_ref[...]).astype(BF16)
    raw_ref[slot] = jnp.dot(xn, wqkv_ref[...], preferred_element_type=F32)


def _mlp_qkv_layer(x, g, w_up, w_down, g_attn, w_qkv, gq, gk):
    b, s, d = x.shape
    ff = w_up.shape[1]
    tm = min(512, s)
    per_b = s // tm
    nt = b * per_b
    cos, sin = _rope_tables(s)
    qkv_dim = w_qkv.shape[1]
    n_qk = (N_HEADS + N_KV_HEADS) * HEAD_DIM
    order = _head_lane_order()
    heads = np.arange(N_HEADS + N_KV_HEADS)[:, None] * HEAD_DIM
    cols = np.concatenate([(heads + order[None, :]).ravel(), np.arange(n_qk, qkv_dim)])
    w = w_qkv[:, cols].astype(BF16)
    q_scale = HEAD_DIM ** -0.5 * np.log2(np.e)
    cur = lambda i: jnp.minimum(i, nt - 1)
    prev = lambda i: jnp.maximum(i - 1, 0)
    row = pl.BlockSpec((tm, d), lambda i: (cur(i), 0))
    head = lambda nh, width: pl.BlockSpec(
        (1, nh, tm, width), lambda i: (prev(i) // per_b, 0, prev(i) % per_b, 0))
    tab = pl.BlockSpec((tm, HEAD_DIM), lambda i: (prev(i) % per_b, 0))
    x2, q, k, v = pl.pallas_call(
        functools.partial(_mlp_qkv_kernel, ff_chunk=min(1024, ff)),
        grid=(nt + 1,),
        in_specs=[
            row, _resident((1, d)), _resident((d, ff)), _resident((ff, d)),
            _resident((1, d)), _resident((d, qkv_dim)),
            _resident((1, HEAD_DIM)), _resident((1, HEAD_DIM)), tab, tab,
        ],
        out_specs=[row, head(N_HEADS, HEAD_DIM), head(N_KV_HEADS, HEAD_DIM),
                   head(N_KV_HEADS, 2 * HEAD_DIM)],
        out_shape=[
            jax.ShapeDtypeStruct((b * s, d), F32),
            jax.ShapeDtypeStruct((b, N_HEADS, s, HEAD_DIM), BF16),
            jax.ShapeDtypeStruct((b, N_KV_HEADS, s, HEAD_DIM), BF16),
            jax.ShapeDtypeStruct((b, N_KV_HEADS, s, 2 * HEAD_DIM), BF16),
        ],
        scratch_shapes=[pltpu.VMEM((2, tm, qkv_dim), F32)],
        compiler_params=_params(1),
        name="mlp_qkv",
    )(x.reshape(b * s, d), g.reshape(1, d), w_up.astype(BF16), w_down.astype(BF16),
      g_attn.reshape(1, d), w, (gq[order] * q_scale).reshape(1, HEAD_DIM),
      gk[order].reshape(1, HEAD_DIM), jnp.asarray(cos), jnp.asarray(sin))
    return x2.reshape(b, s, d), q, k, v


def _attn_kernel(q_ref, k_ref, v_ref, o_ref, s0, s1, p0, p1, *, tq, nq):
    s_buf, p_buf = (s0, s1), (p0, p1)
    seq = k_ref.shape[2]

    def scores(row, h, slot):
        q = q_ref[0, h, pl.ds(row, tq), :]
        s_buf[slot][:, :seq] = lax.dot_general(q, k_ref[0, 0], (((1,), (1,)), ((), ())),
                                               preferred_element_type=F32)

    def softmax(slot):
        sb, pb = s_buf[slot], p_buf[slot]
        m = sb[:, 0:LANES]
        for c in range(1, seq // LANES):
            m = jnp.maximum(m, sb[:, c * LANES:(c + 1) * LANES])
        mb = jnp.broadcast_to(jnp.max(m, axis=-1, keepdims=True), (tq, LANES))
        for c in range(seq // LANES):
            p = jnp.exp2(sb[:, c * LANES:(c + 1) * LANES] - mb)
            pb[:, c * LANES:(c + 1) * LANES] = p.astype(BF16)

    def weighted_values(row, h, slot):
        o = jnp.dot(p_buf[slot][:, :seq], v_ref[0, 0], preferred_element_type=F32)
        o = o[:, :HEAD_DIM] / o[:, HEAD_DIM:HEAD_DIM + 1]
        o_ref[0, pl.ds(row, tq), h * HEAD_DIM:(h + 1) * HEAD_DIM] = o.astype(BF16)

    p1[...] = jnp.ones_like(p1)
    scores(0, 0, 0)

    def tile_ticks(qi):
        row = pl.multiple_of(qi * tq, tq)
        row_next = pl.multiple_of(jnp.minimum(qi + 1, nq - 1) * tq, tq)
        row_prev = pl.multiple_of(jnp.maximum(qi - 1, 0) * tq, tq)
        for h in range(KV_GROUP):
            slot = h % 2
            if h + 1 < KV_GROUP:
                scores(row, h + 1, 1 - slot)
            else:
                scores(row_next, 0, 1 - slot)
            softmax(slot)
            if h > 0:
                weighted_values(row, h - 1, 1 - slot)
            else:
                weighted_values(row_prev, KV_GROUP - 1, 1 - slot)

    def body(qj, carry):
        for u in range(TILES_PER_TRIP):
            tile_ticks(qj * TILES_PER_TRIP + u)
        return carry

    lax.fori_loop(0, nq // TILES_PER_TRIP, body, 0)
    weighted_values((nq - 1) * tq, KV_GROUP - 1, 1)


def _attention(q, k, v):
    b, _, s, _ = q.shape
    tq = min(128, s)
    nq = s // tq
    assert s % tq == 0 and nq % TILES_PER_TRIP == 0
    whole =lambda nh, width: pl.BlockSpec((1, nh, s, width), lambda i, g: (i, g, 0, 0))
    return pl.pallas_call(
        functools.partial(_attn_kernel, tq=tq, nq=nq),
        grid=(b, N_KV_HEADS),
        in_specs=[whole(KV_GROUP, HEAD_DIM), whole(1, HEAD_DIM), whole(1, 2 * HEAD_DIM)],
        out_specs=pl.BlockSpec((1, s, KV_GROUP * HEAD_DIM), lambda i, g: (i, 0, g)),
        out_shape=jax.ShapeDtypeStruct((b, s, N_HEADS * HEAD_DIM), BF16),
        scratch_shapes=[pltpu.VMEM((tq, s + LANES), F32), pltpu.VMEM((tq, s + LANES), F32),
                        pltpu.VMEM((tq, s + LANES), BF16), pltpu.VMEM((tq, s + LANES), BF16)],
        compiler_params=_params(2),
        name="attention",
    )(q, k, v)


def _trunk(x, fourier_norm, fourier_w_out, attn_norm, attn_w_qkv, attn_q_norm, attn_k_norm,
           attn_w_o, mlp_norm, mlp_w_up, mlp_w_down, final_norm):
    b, s, d = x.shape
    depth = mlp_norm.shape[0]
    qkv = None
    for i in range(depth):
        j = i // 2
        g_final = final_norm if i == depth - 1 else None
        if i % 2 == 0:
            x = _fourier_layer(x, fourier_norm[j], fourier_w_out[j])
            if i + 1 < depth:
                x, *qkv = _mlp_qkv_layer(x, mlp_norm[i], mlp_w_up[i], mlp_w_down[i], attn_norm[j],
                                         attn_w_qkv[j], attn_q_norm[j], attn_k_norm[j])
            else:
                x = _mlp_layer(x.reshape(b * s, d), mlp_norm[i], mlp_w_up[i], mlp_w_down[i],
                               g_final=g_final).reshape(b, s, d)
        else:
            a = _attention(*qkv)
            x = _mlp_layer(x.reshape(b * s, d), mlp_norm[i], mlp_w_up[i], mlp_w_down[i],
                           attn=a.reshape(b * s, d), w_o=attn_w_o[j],
                           g_final=g_final).reshape(b, s, d)
    return x


def kernel(x_prompt, x_sample, fourier_norm, fourier_w_out, attn_norm, attn_w_qkv, attn_q_norm,
           attn_k_norm, attn_w_o, mlp_norm, mlp_w_up, mlp_w_down, final_norm):
    weights = (fourier_norm, fourier_w_out, attn_norm, attn_w_qkv, attn_q_norm, attn_k_norm,
               attn_w_o, mlp_norm, mlp_w_up, mlp_w_down, final_norm)
    return (_trunk(x_prompt, *weights), _trunk(x_sample, *weights))
```

```python
import functools

import numpy as np
import jax
import jax.numpy as jnp
from jax import lax
from jax.experimental import pallas as pl
from jax.experimental.pallas import tpu as pltpu

F32 = jnp.float32
BF16 = jnp.bfloat16

N_FGROUPS = 8
HEAD_DIM = 128
N_HEADS = 8
N_KV_HEADS = 2
KV_GROUP = N_HEADS // N_KV_HEADS
AXIS_DIM = HEAD_DIM // 2
ROPE_THETA = 10000.0
GRID_W = 64
EPS = 1e-6

V7X_VMEM_BYTES = 64 * 1024 * 1024
VMEM_LIMIT_BYTES = V7X_VMEM_BYTES - 8 * 1024 * 1024
LANES = 128

RADIX = 8
FOURIER_ROWS = 512
TILES_PER_TRIP = 4


def _params(n_grid_dims):
    return pltpu.CompilerParams(
        dimension_semantics=("arbitrary",) * n_grid_dims,
        vmem_limit_bytes=VMEM_LIMIT_BYTES,
    )


def _rms(x, g):
    return x * lax.rsqrt(jnp.mean(x * x, axis=-1, keepdims=True) + EPS) * g


def _resident(shape):
    nd = len(shape)
    return pl.BlockSpec(shape, lambda *_: (0,) * nd, pipeline_mode=pl.Buffered(1))


def _fourier_tables(s, d, tk):
    n = s // RADIX
    k = np.arange(n)
    ang = 2.0 * np.pi * ((k[:, None] * k[None, :]) % n) / n
    c = np.cos(ang).reshape(n // tk, tk, n)
    sn = np.sin(ang).reshape(n // tk, tk, n)
    dft = np.concatenate([c, -sn], axis=1).astype(np.float32)
    tw = np.zeros((n, 2 * RADIX), np.float32)
    for j in range(1, RADIX):
        th = 2.0 * np.pi * ((k * j) % s) / s
        tw[:, 2 * (j - 1)] = np.cos(th)
        tw[:, 2 * (j - 1) + 1] = -np.sin(th)
    fg = d // N_FGROUPS
    cc = np.arange(fg)
    cang = 2.0 * np.pi * ((cc[:, None] * cc[None, :]) % fg) / fg
    scale = 1.0 / np.sqrt(float(s) * fg)
    chan = np.concatenate([np.cos(cang), np.sin(cang)], axis=0) * scale
    return dft, tw, chan.astype(np.float32)


def _times_root(z, m, r):
    re, im = z
    m %= r
    if m == 0:
        return re, im
    if 4 * m == r:
        return im, -re
    if 2 * m == r:
        return -re, -im
    if 4 * m == 3 * r:
        return -im, re
    wr, wi = float(np.cos(2.0 * np.pi * m / r)), float(-np.sin(2.0 * np.pi * m / r))
    if 8 * m % r == 0:
        a = abs(wr)
        sr, si = np.sign(wr), np.sign(wi)
        return (sr * re - si * im) * a, (si * re + sr * im) * a
    return re * wr - im * wi, re * wi + im * wr


def _small_dft(zs):
    r = len(zs)
    if r == 1:
        return zs
    even, odd = _small_dft(zs[0::2]), _small_dft(zs[1::2])
    out = [None] * r
    for q in range(r // 2):
        tr, ti = _times_root(odd[q], q, r)
        out[q] = (even[q][0] + tr, even[q][1] + ti)
        out[q + r // 2] = (even[q][0] - tr, even[q][1] - ti)
    return out


def _normalise_rows(xt_ref, g_ref, cols_ref, xr_ref, j, d):
    rows = xt_ref.shape[1] // RADIX
    xn = _rms(xt_ref[0], g_ref[...])
    for c in range(d // LANES):
        cols_ref[c] = xn[:, c * LANES:(c + 1) * LANES]
    r0 = pl.multiple_of(j * rows, rows)
    for jr in range(RADIX):
        for c in range(d // LANES):
            xr_ref[pl.ds(r0, rows), jr * d + c * LANES:jr * d + (c + 1) * LANES] = (
                cols_ref[c, pl.ds(jr, rows, stride=RADIX), :].astype(BF16))


def _fourier_tile(dft_ref, xr_ref, tw_ref, chan_ref, w_ref, x_ref, o_ref, tk, d):
    e = jnp.dot(dft_ref[0], xr_ref[...], preferred_element_type=F32)
    tw = tw_ref[...]
    zs = [(e[:tk, 0:d], e[tk:, 0:d])]
    for j in range(1, RADIX):
        er, ei = e[:tk, j * d:(j + 1) * d], e[tk:, j * d:(j + 1) * d]
        tr = tw[:, 2 * (j - 1):2 * (j - 1) + 1]
        ti = tw[:, 2 * (j - 1) + 1:2 * (j - 1) + 2]
        zs.append((er * tr - ei * ti, er * ti + ei * tr))
    us = _small_dft(zs)
    ur = jnp.concatenate([u[0] for u in us], axis=0).astype(BF16)
    ui = jnp.concatenate([u[1] for u in us], axis=0).astype(BF16)
    fg = d // N_FGROUPS
    chan = chan_ref[...]
    f = []
    for g in range(N_FGROUPS):
        lhs = jnp.concatenate([ur[:, g * fg:(g + 1) * fg], ui[:, g * fg:(g + 1) * fg]], axis=1)
        f.append(jnp.dot(lhs, chan, preferred_element_type=F32))
    f = jnp.concatenate(f, axis=1).astype(BF16)
    y = jnp.dot(f, w_ref[...], preferred_element_type=F32)
    o_ref[0] = x_ref[0] + y.reshape(RADIX, tk, d)


def _fourier_kernel(dft_ref, xt_ref, g_ref, tw_ref, chan_ref, w_ref, x_ref, o_ref,
                    xr0, xr1, cols_ref, *, tk, d):
    i, j = pl.program_id(0), pl.program_id(1)

    def step(xr_fill, xr_mix):
        _fourier_tile(dft_ref, xr_mix, tw_ref, chan_ref, w_ref, x_ref, o_ref, tk, d)
        _normalise_rows(xt_ref, g_ref, cols_ref, xr_fill, j, d)

    @pl.when(i == 0)
    def _():
        _normalise_rows(xt_ref, g_ref, cols_ref, xr0, j, d)

    @pl.when((i > 0) & (i % 2 == 0))
    def _():
        step(xr0, xr1)

    @pl.when(i % 2 == 1)
    def _():
        step(xr1, xr0)


def _fourier_layer(x, g, w_out):
    b, s, d = x.shape
    n = s // RADIX
    tk = min(FOURIER_ROWS // RADIX, n)
    nk = n // tk
    tokens = s // nk
    dft, tw, chan = _fourier_tables(s, d, tk)
    dft = jnp.asarray(dft).astype(BF16)
    chan = jnp.asarray(chan).astype(BF16)
    x4 = x.reshape(b, RADIX, n, d)
    fg = d // N_FGROUPS
    fill = lambda i: jnp.minimum(i, b - 1)
    mix = lambda i: jnp.maximum(i - 1, 0)
    blocks = pl.BlockSpec((1, RADIX, tk, d), lambda i, j: (mix(i), 0, jnp.where(i == 0, 0, j), 0))
    out = pl.pallas_call(
        functools.partial(_fourier_kernel, tk=tk, d=d),
        grid=(b + 1, nk),
        in_specs=[
            pl.BlockSpec((1, 2 * tk, n), lambda i, j: (j, 0, 0)),
            pl.BlockSpec((1, tokens, d), lambda i, j: (fill(i), j, 0)),
            _resident((1, d)),
            pl.BlockSpec((tk, 2 * RADIX), lambda i, j: (j, 0)),
            _resident((2 * fg, fg)),
            _resident((d, d)),
            blocks,
        ],
        out_specs=blocks,
        out_shape=jax.ShapeDtypeStruct((b, RADIX, n, d), F32),
        scratch_shapes=[pltpu.VMEM((n, RADIX * d), BF16), pltpu.VMEM((n, RADIX * d), BF16),
                        pltpu.VMEM((d // LANES, tokens, LANES), F32)],
        compiler_params=_params(2),
        name="fourier",
    )(dft, x, g.reshape(1, d), jnp.asarray(tw), chan, w_out.astype(BF16), x4)
    return out.reshape(b, s, d)


def _head_lane_order():
    quarter = AXIS_DIM // 2
    blocks = np.arange(HEAD_DIM).reshape(4, quarter)
    return np.concatenate([blocks[0], blocks[2], blocks[1], blocks[3]])


def _rope_tables(s):
    pos = np.arange(s)
    inv = ROPE_THETA ** (-np.arange(0, AXIS_DIM, 2, dtype=np.float64) / AXIS_DIM)
    ang = np.concatenate([(pos // GRID_W)[:, None] * inv[None, :],
                          (pos % GRID_W)[:, None] * inv[None, :]], axis=1)
    cos = np.concatenate([np.cos(ang), np.cos(ang)], axis=1)
    sin = np.concatenate([-np.sin(ang), np.sin(ang)], axis=1)
    return cos.astype(np.float32), sin.astype(np.float32)


def _rope(x, cos, sin):
    return x * cos + pltpu.roll(x, HEAD_DIM // 2, 1) * sin


def _store_kv_heads(raw, gk_ref, cos_ref, sin_ref, k_ref, v_ref):
    cos, sin = cos_ref[...], sin_ref[...]
    lane = lax.broadcasted_iota(jnp.int32, cos.shape, 1)
    ones_col = jnp.where(lane == 0, 1.0, 0.0).astype(BF16)
    for h in range(N_KV_HEADS):
        kh = _rms(raw[:, h * HEAD_DIM:(h + 1) * HEAD_DIM], gk_ref[...])
        k_ref[0, h] = _rope(kh, cos, sin).astype(BF16)
        off = (N_KV_HEADS + h) * HEAD_DIM
        v_ref[0, h, :, :HEAD_DIM] = raw[:, off:off + HEAD_DIM].astype(BF16)
        v_ref[0, h, :, HEAD_DIM:] = ones_col


def _mlp_block(x, g_ref, wup_ref, wdown_ref, ff_chunk):
    xn = _rms(x, g_ref[...]).astype(BF16)
    acc = x
    for c in range(wup_ref.shape[1] // ff_chunk):
        h = jnp.dot(xn, wup_ref[:, c * ff_chunk:(c + 1) * ff_chunk], preferred_element_type=F32)
        h = jnp.maximum(h, 0.0)
        h = (h * h).astype(BF16)
        acc = acc + jnp.dot(h, wdown_ref[c * ff_chunk:(c + 1) * ff_chunk, :],
                            preferred_element_type=F32)
    return acc


def _mlp_kernel(*refs, has_proj, has_final, ff_chunk):
    refs = list(refs)
    x_ref = refs.pop(0)
    if has_proj:
        a_ref, wo_ref = refs.pop(0), refs.pop(0)
    g_ref, wup_ref, wdown_ref = refs.pop(0), refs.pop(0), refs.pop(0)
    if has_final:
        gf_ref = refs.pop(0)
    o_ref = refs.pop(0)

    x = x_ref[...]
    if has_proj:
        x = x + jnp.dot(a_ref[...], wo_ref[...], preferred_element_type=F32)
    acc = _mlp_block(x, g_ref, wup_ref, wdown_ref, ff_chunk)
    if has_final:
        acc = _rms(acc, gf_ref[...])
    o_ref[...] = acc


def _mlp_layer(x2d, g, w_up, w_down, attn=None, w_o=None, g_final=None):
    t, d = x2d.shape
    ff = w_up.shape[1]
    tm = min(512, t)
    has_proj = attn is not None
    has_final = g_final is not None
    row = pl.BlockSpec((tm, d), lambda i: (i, 0))
    args, specs = [x2d], [row]
    if has_proj:
        args += [attn, w_o.astype(BF16)]
        specs += [row, _resident((d, d))]
    args += [g.reshape(1, d), w_up.astype(BF16), w_down.astype(BF16)]
    specs += [_resident((1, d)), _resident((d, ff)), _resident((ff, d))]
    if has_final:
        args.append(g_final.reshape(1, d))
        specs.append(_resident((1, d)))
    return pl.pallas_call(
        functools.partial(_mlp_kernel, has_proj=has_proj, has_final=has_final,
                          ff_chunk=min(1024, ff)),
        grid=(t // tm,),
        in_specs=specs,
        out_specs=row,
        out_shape=jax.ShapeDtypeStruct((t, d), F32),
        compiler_params=_params(1),
        name="mlp_proj" if has_proj else "mlp",
    )(*args)


def _mlp_qkv_kernel(x_ref, g_ref, wup_ref, wdown_ref, ga_ref, wqkv_ref, gk_ref,
                    cos_ref, sin_ref, o_ref, q_ref, k_ref, v_ref, raw_ref, *, ff_chunk):
    i = pl.program_id(0)
    slot = i % 2
    n_q = q_ref.shape[1]

    @pl.when(i == 0)
    def _():
        raw_ref[1] = jnp.zeros(raw_ref.shape[1:], F32)

    _store_kv_heads(raw_ref.at[1 - slot], gk_ref, cos_ref, sin_ref, k_ref, v_ref)
    acc = _mlp_block(x_ref[...], g_ref, wup_ref, wdown_ref, ff_chunk)
    o_ref[...] = acc
    xn = _rms(acc, ga_ref[...]).astype(BF16)
    qkv = jnp.dot(xn, wqkv_ref[...], preferred_element_type=F32)
    q_ref[...] = qkv[:, :n_q]
    raw_ref[slot] = qkv[:, n_q:]


def _qk_gains(gq, gk):
    order = _head_lane_order()
    q_scale = HEAD_DIM ** -0.5 * np.log2(np.e)
    return (gq[order] * q_scale).reshape(1, HEAD_DIM), gk[order].reshape(1, HEAD_DIM)


def _mlp_qkv_layer(x, g, w_up, w_down, g_attn, w_qkv, gk):
    b, s, d = x.shape
    ff = w_up.shape[1]
    tm = min(512, s)
    per_b = s // tm
    nt = b * per_b
    cos, sin = _rope_tables(s)
    qkv_dim = w_qkv.shape[1]
    n_q = N_HEADS * HEAD_DIM
    n_qk = (N_HEADS + N_KV_HEADS) * HEAD_DIM
    heads = np.arange(N_HEADS + N_KV_HEADS)[:, None] * HEAD_DIM
    cols = np.concatenate([(heads + _head_lane_order()[None, :]).ravel(), np.arange(n_qk, qkv_dim)])
    w = w_qkv[:, cols].astype(BF16)
    cur = lambda i: jnp.minimum(i, nt - 1)
    prev = lambda i: jnp.maximum(i - 1, 0)
    row = lambda width: pl.BlockSpec((tm, width), lambda i: (cur(i), 0))
    head = lambda width: pl.BlockSpec(
        (1, N_KV_HEADS, tm, width), lambda i: (prev(i) // per_b, 0, prev(i) % per_b, 0))
    tab = pl.BlockSpec((tm, HEAD_DIM), lambda i: (prev(i) % per_b, 0))
    x2, q, k, v = pl.pallas_call(
        functools.partial(_mlp_qkv_kernel, ff_chunk=min(1024, ff)),
        grid=(nt + 1,),
        in_specs=[
            row(d), _resident((1, d)), _resident((d, ff)), _resident((ff, d)),
            _resident((1, d)), _resident((d, qkv_dim)), _resident((1, HEAD_DIM)), tab, tab,
        ],
        out_specs=[row(d), row(n_q), head(HEAD_DIM), head(2 * HEAD_DIM)],
        out_shape=[
            jax.ShapeDtypeStruct((b * s, d), F32),
            jax.ShapeDtypeStruct((b * s, n_q), F32),
            jax.ShapeDtypeStruct((b, N_KV_HEADS, s, HEAD_DIM), BF16),
            jax.ShapeDtypeStruct((b, N_KV_HEADS, s, 2 * HEAD_DIM), BF16),
        ],
        scratch_shapes=[pltpu.VMEM((2, tm, qkv_dim - n_q), F32)],
        compiler_params=_params(1),
        name="mlp_qkv",
    )(x.reshape(b * s, d), g.reshape(1, d), w_up.astype(BF16), w_down.astype(BF16),
      g_attn.reshape(1, d), w, gk, jnp.asarray(cos), jnp.asarray(sin))
    return x2.reshape(b, s, d), q.reshape(b, s, n_q), k, v


def _attn_kernel(q_ref, gq_ref, cos_ref, sin_ref, k_ref, v_ref, o_ref, s0, s1, p0, p1, *, tq, nq):
    s_buf, p_buf = (s0, s1), (p0, p1)
    seq = k_ref.shape[2]

    def scores(row, h, slot):
        q = _rms(q_ref[0, pl.ds(row, tq), h * HEAD_DIM:(h + 1) * HEAD_DIM], gq_ref[...])
        q = _rope(q, cos_ref[pl.ds(row, tq), :], sin_ref[pl.ds(row, tq), :]).astype(BF16)
        s_buf[slot][:, :seq] = lax.dot_general(q, k_ref[0, 0], (((1,), (1,)), ((), ())),
                                               preferred_element_type=F32)

    def softmax(slot):
        sb, pb = s_buf[slot], p_buf[slot]
        m = sb[:, 0:LANES]
        for c in range(1, seq // LANES):
            m = jnp.maximum(m, sb[:, c * LANES:(c + 1) * LANES])
        mb = jnp.broadcast_to(jnp.max(m, axis=-1, keepdims=True), (tq, LANES))
        for c in range(seq // LANES):
            p = jnp.exp2(sb[:, c * LANES:(c + 1) * LANES] - mb)
            pb[:, c * LANES:(c + 1) * LANES] = p.astype(BF16)

    def weighted_values(row, h, slot):
        o = jnp.dot(p_buf[slot][:, :seq], v_ref[0, 0], preferred_element_type=F32)
        o = o[:, :HEAD_DIM] / o[:, HEAD_DIM:HEAD_DIM + 1]
        o_ref[0, pl.ds(row, tq), h * HEAD_DIM:(h + 1) * HEAD_DIM] = o.astype(BF16)

    p1[...] = jnp.ones_like(p1)
    scores(0, 0, 0)

    def tile_ticks(qi):
        row = pl.multiple_of(qi * tq, tq)
        row_next = pl.multiple_of(jnp.minimum(qi + 1, nq - 1) * tq, tq)
        row_prev = pl.multiple_of(jnp.maximum(qi - 1, 0) * tq, tq)
        for h in range(KV_GROUP):
            slot = h % 2
            if h + 1 < KV_GROUP:
                scores(row, h + 1, 1 - slot)
            else:
                scores(row_next, 0, 1 - slot)
            softmax(slot)
            if h > 0:
                weighted_values(row, h - 1, 1 - slot)
            else:
                weighted_values(row_prev, KV_GROUP - 1, 1 - slot)

    def body(qj, carry):
        for u in range(TILES_PER_TRIP):
            tile_ticks(qj * TILES_PER_TRIP + u)
        return carry

    lax.fori_loop(0, nq // TILES_PER_TRIP, body, 0)
    weighted_values((nq - 1) * tq, KV_GROUP - 1, 1)


def _attention(q, gq, k, v):
    b, s, _ = q.shape
    tq = min(128, s)
    nq = s // tq
    assert s % tq == 0 and nq % TILES_PER_TRIP == 0
    cos, sin = _rope_tables(s)
    whole = lambda width: pl.BlockSpec((1, 1, s, width), lambda i, g: (i, g, 0, 0))
    group = pl.BlockSpec((1, s, KV_GROUP * HEAD_DIM), lambda i, g: (i, 0, g))
    return pl.pallas_call(
        functools.partial(_attn_kernel, tq=tq, nq=nq),
        grid=(b, N_KV_HEADS),
        in_specs=[group, _resident((1, HEAD_DIM)), _resident((s, HEAD_DIM)), _resident((s, HEAD_DIM)),
                  whole(HEAD_DIM), whole(2 * HEAD_DIM)],
        out_specs=group,
        out_shape=jax.ShapeDtypeStruct((b, s, N_HEADS * HEAD_DIM), BF16),
        scratch_shapes=[pltpu.VMEM((tq, s + LANES), F32), pltpu.VMEM((tq, s + LANES), F32),
                        pltpu.VMEM((tq, s + LANES), BF16), pltpu.VMEM((tq, s + LANES), BF16)],
        compiler_params=_params(2),
        name="attention",
    )(q, gq, jnp.asarray(cos), jnp.asarray(sin), k, v)


def _trunk(x, fourier_norm, fourier_w_out, attn_norm, attn_w_qkv, attn_q_norm, attn_k_norm,
           attn_w_o, mlp_norm, mlp_w_up, mlp_w_down, final_norm):
    b, s, d = x.shape
    depth = mlp_norm.shape[0]
    qkv = None
    for i in range(depth):
        j = i // 2
        g_final = final_norm if i == depth - 1 else None
        if i % 2 == 0:
            x = _fourier_layer(x, fourier_norm[j], fourier_w_out[j])
            if i + 1 < depth:
                gq, gk = _qk_gains(attn_q_norm[j], attn_k_norm[j])
                x, q, k, v = _mlp_qkv_layer(x, mlp_norm[i], mlp_w_up[i], mlp_w_down[i], attn_norm[j],
                                            attn_w_qkv[j], gk)
                qkv = (q, gq, k, v)
            else:
                x = _mlp_layer(x.reshape(b * s, d), mlp_norm[i], mlp_w_up[i], mlp_w_down[i],
                               g_final=g_final).reshape(b, s, d)
        else:
            a = _attention(*qkv)
            x = _mlp_layer(x.reshape(b * s, d), mlp_norm[i], mlp_w_up[i], mlp_w_down[i],
                           attn=a.reshape(b * s, d), w_o=attn_w_o[j],
                           g_final=g_final).reshape(b, s, d)
    return x


def kernel(x_prompt, x_sample, fourier_norm, fourier_w_out, attn_norm, attn_w_qkv, attn_q_norm,
           attn_k_norm, attn_w_o, mlp_norm, mlp_w_up, mlp_w_down, final_norm):
    weights = (fourier_norm, fourier_w_out, attn_norm, attn_w_qkv, attn_q_norm, attn_k_norm,
               attn_w_o, mlp_norm, mlp_w_up, mlp_w_down, final_norm)
    return (_trunk(x_prompt, *weights), _trunk(x_sample, *weights))
```

```python
import functools

import numpy as np
import jax
import jax.numpy as jnp
from jax import lax
from jax.experimental import pallas as pl
from jax.experimental.pallas import tpu as pltpu

F32 = jnp.float32
BF16 = jnp.bfloat16

N_FGROUPS = 8
HEAD_DIM = 128
N_HEADS = 8
N_KV_HEADS = 2
KV_GROUP = N_HEADS // N_KV_HEADS
AXIS_DIM = HEAD_DIM // 2
ROPE_THETA = 10000.0
GRID_W = 64
EPS = 1e-6

V7X_VMEM_BYTES = 64 * 1024 * 1024
VMEM_LIMIT_BYTES = V7X_VMEM_BYTES - 8 * 1024 * 1024
LANES = 128

RADIX = 8
FOURIER_ROWS = 512
MLP_ROWS = 512
FF_CHUNK = 1024
ATTN_ROWS = 128
TILES_PER_TRIP = 4


def _params(n_grid_dims):
    return pltpu.CompilerParams(
        dimension_semantics=("arbitrary",) * n_grid_dims,
        vmem_limit_bytes=VMEM_LIMIT_BYTES,
    )


def _rms(x, g):
    return x * lax.rsqrt(jnp.mean(x * x, axis=-1, keepdims=True) + EPS) * g


def _resident(shape):
    nd = len(shape)
    return pl.BlockSpec(shape, lambda *_: (0,) * nd, pipeline_mode=pl.Buffered(1))


def _fourier_tables(s, d, tk):
    n = s // RADIX
    k = np.arange(n)
    ang = 2.0 * np.pi * ((k[:, None] * k[None, :]) % n) / n
    c = np.cos(ang).reshape(n // tk, tk, n)
    sn = np.sin(ang).reshape(n // tk, tk, n)
    dft = np.concatenate([c, -sn], axis=1).astype(np.float32)
    tw = np.zeros((n, 2 * RADIX), np.float32)
    for j in range(1, RADIX):
        th = 2.0 * np.pi * ((k * j) % s) / s
        tw[:, 2 * (j - 1)] = np.cos(th)
        tw[:, 2 * (j - 1) + 1] = -np.sin(th)
    fg = d // N_FGROUPS
    cc = np.arange(fg)
    cang = 2.0 * np.pi * ((cc[:, None] * cc[None, :]) % fg) / fg
    scale = 1.0 / np.sqrt(float(s) * fg)
    chan = np.concatenate([np.cos(cang), np.sin(cang)], axis=0) * scale
    return dft, tw, chan.astype(np.float32)


def _times_root(z, m, r):
    re, im = z
    m %= r
    if m == 0:
        return re, im
    if 4 * m == r:
        return im, -re
    if 2 * m == r:
        return -re, -im
    if 4 * m == 3 * r:
        return -im, re
    wr, wi = float(np.cos(2.0 * np.pi * m / r)), float(-np.sin(2.0 * np.pi * m / r))
    if 8 * m % r == 0:
        a = abs(wr)
        sr, si = np.sign(wr), np.sign(wi)
        return (sr * re - si * im) * a, (si * re + sr * im) * a
    return re * wr - im * wi, re * wi + im * wr


def _small_dft(zs):
    r = len(zs)
    if r == 1:
        return zs
    even, odd = _small_dft(zs[0::2]), _small_dft(zs[1::2])
    out = [None] * r
    for q in range(r // 2):
        tr, ti = _times_root(odd[q], q, r)
        out[q] = (even[q][0] + tr, even[q][1] + ti)
        out[q + r // 2] = (even[q][0] - tr, even[q][1] - ti)
    return out


def _normalise_rows(xt_ref, g_ref, cols_ref, xr_ref, j, d):
    rows = xt_ref.shape[1] // RADIX
    xn = _rms(xt_ref[0], g_ref[...])
    for c in range(d // LANES):
        cols_ref[c] = xn[:, c * LANES:(c + 1) * LANES]
    r0 = pl.multiple_of(j * rows, rows)
    for jr in range(RADIX):
        for c in range(d // LANES):
            xr_ref[pl.ds(r0, rows), jr * d + c * LANES:jr * d + (c + 1) * LANES] = (
                cols_ref[c, pl.ds(jr, rows, stride=RADIX), :].astype(BF16))


def _fourier_tile(dft_ref, xr_ref, tw_ref, chan_ref, w_ref, x_ref, o_ref, tk, d):
    e = jnp.dot(dft_ref[0], xr_ref[...], preferred_element_type=F32)
    tw = tw_ref[...]
    zs = [(e[:tk, 0:d], e[tk:, 0:d])]
    for j in range(1, RADIX):
        er, ei = e[:tk, j * d:(j + 1) * d], e[tk:, j * d:(j + 1) * d]
        tr = tw[:, 2 * (j - 1):2 * (j - 1) + 1]
        ti = tw[:, 2 * (j - 1) + 1:2 * (j - 1) + 2]
        zs.append((er * tr - ei * ti, er * ti + ei * tr))
    us = _small_dft(zs)
    ur = jnp.concatenate([u[0] for u in us], axis=0).astype(BF16)
    ui = jnp.concatenate([u[1] for u in us], axis=0).astype(BF16)
    fg = d // N_FGROUPS
    chan = chan_ref[...]
    f = []
    for g in range(N_FGROUPS):
        lhs = jnp.concatenate([ur[:, g * fg:(g + 1) * fg], ui[:, g * fg:(g + 1) * fg]], axis=1)
        f.append(jnp.dot(lhs, chan, preferred_element_type=F32))
    f = jnp.concatenate(f, axis=1).astype(BF16)
    y = jnp.dot(f, w_ref[...], preferred_element_type=F32)
    o_ref[0] = x_ref[0] + y.reshape(RADIX, tk, d)


def _fourier_kernel(dft_ref, xt_ref, g_ref, tw_ref, chan_ref, w_ref, x_ref, o_ref,
                    xr0, xr1, cols_ref, *, tk, d):
    i, j = pl.program_id(0), pl.program_id(1)

    def step(xr_fill, xr_mix):
        _fourier_tile(dft_ref, xr_mix, tw_ref, chan_ref, w_ref, x_ref, o_ref, tk, d)
        _normalise_rows(xt_ref, g_ref, cols_ref, xr_fill, j, d)

    @pl.when(i == 0)
    def _():
        _normalise_rows(xt_ref, g_ref, cols_ref, xr0, j, d)

    @pl.when((i > 0) & (i % 2 == 0))
    def _():
        step(xr0, xr1)

    @pl.when(i % 2 == 1)
    def _():
        step(xr1, xr0)


def _fourier_layer(x, g, w_out):
    b, s, d = x.shape
    n = s // RADIX
    tk = min(FOURIER_ROWS // RADIX, n)
    nk = n // tk
    tokens = s // nk
    dft, tw, chan = _fourier_tables(s, d, tk)
    dft = jnp.asarray(dft).astype(BF16)
    chan = jnp.asarray(chan).astype(BF16)
    x4 = x.reshape(b, RADIX, n, d)
    fg = d // N_FGROUPS
    fill = lambda i: jnp.minimum(i, b - 1)
    mix = lambda i: jnp.maximum(i - 1, 0)
    blocks = pl.BlockSpec((1, RADIX, tk, d), lambda i, j: (mix(i), 0, jnp.where(i == 0, 0, j), 0))
    out = pl.pallas_call(
        functools.partial(_fourier_kernel, tk=tk, d=d),
        grid=(b + 1, nk),
        in_specs=[
            pl.BlockSpec((1, 2 * tk, n), lambda i, j: (j, 0, 0)),
            pl.BlockSpec((1, tokens, d), lambda i, j: (fill(i), j, 0)),
            _resident((1, d)),
            pl.BlockSpec((tk, 2 * RADIX), lambda i, j: (j, 0)),
            _resident((2 * fg, fg)),
            _resident((d, d)),
            blocks,
        ],
        out_specs=blocks,
        out_shape=jax.ShapeDtypeStruct((b, RADIX, n, d), F32),
        scratch_shapes=[pltpu.VMEM((n, RADIX * d), BF16), pltpu.VMEM((n, RADIX * d), BF16),
                        pltpu.VMEM((d // LANES, tokens, LANES), F32)],
        compiler_params=_params(2),
        name="fourier",
    )(dft, x, g.reshape(1, d), jnp.asarray(tw), chan, w_out.astype(BF16), x4)
    return out.reshape(b, s, d)


def _head_lane_order():
    quarter = AXIS_DIM // 2
    blocks = np.arange(HEAD_DIM).reshape(4, quarter)
    return np.concatenate([blocks[0], blocks[2], blocks[1], blocks[3]])


def _rope_tables(s):
    pos = np.arange(s)
    inv = ROPE_THETA ** (-np.arange(0, AXIS_DIM, 2, dtype=np.float64) / AXIS_DIM)
    ang = np.concatenate([(pos // GRID_W)[:, None] * inv[None, :],
                          (pos % GRID_W)[:, None] * inv[None, :]], axis=1)
    cos = np.concatenate([np.cos(ang), np.cos(ang)], axis=1)
    sin = np.concatenate([-np.sin(ang), np.sin(ang)], axis=1)
    return cos.astype(np.float32), sin.astype(np.float32)


def _rope(x, cos, sin):
    return x * cos + pltpu.roll(x, HEAD_DIM // 2, 1) * sin


def _store_kv_heads(raw, gk_ref, cos_ref, sin_ref, k_ref, v_ref):
    cos, sin = cos_ref[...], sin_ref[...]
    lane = lax.broadcasted_iota(jnp.int32, cos.shape, 1)
    ones_col = jnp.where(lane == 0, 1.0, 0.0).astype(BF16)
    for h in range(N_KV_HEADS):
        kh = _rms(raw[:, h * HEAD_DIM:(h + 1) * HEAD_DIM], gk_ref[...])
        k_ref[0, h] = _rope(kh, cos, sin).astype(BF16)
        off = (N_KV_HEADS + h) * HEAD_DIM
        v_ref[0, h, :, :HEAD_DIM] = raw[:, off:off + HEAD_DIM].astype(BF16)
        v_ref[0, h, :, HEAD_DIM:] = ones_col


def _mlp_block(x, g_ref, wup_ref, wdown_ref, ff_chunk):
    xn = _rms(x, g_ref[...]).astype(BF16)
    acc = x
    for c in range(wup_ref.shape[1] // ff_chunk):
        h = jnp.dot(xn, wup_ref[:, c * ff_chunk:(c + 1) * ff_chunk], preferred_element_type=F32)
        h = jnp.maximum(h, 0.0)
        h = (h * h).astype(BF16)
        acc = acc + jnp.dot(h, wdown_ref[c * ff_chunk:(c + 1) * ff_chunk, :],
                            preferred_element_type=F32)
    return acc


def _mlp_kernel(*refs, has_proj, has_final, ff_chunk):
    refs = list(refs)
    x_ref = refs.pop(0)
    if has_proj:
        a_ref, wo_ref = refs.pop(0), refs.pop(0)
    g_ref, wup_ref, wdown_ref = refs.pop(0), refs.pop(0), refs.pop(0)
    if has_final:
        gf_ref = refs.pop(0)
    o_ref = refs.pop(0)

    x = x_ref[...]
    if has_proj:
        x = x + jnp.dot(a_ref[...], wo_ref[...], preferred_element_type=F32)
    acc = _mlp_block(x, g_ref, wup_ref, wdown_ref, ff_chunk)
    if has_final:
        acc = _rms(acc, gf_ref[...])
    o_ref[...] = acc


def _mlp_layer(x2d, g, w_up, w_down, attn=None, w_o=None, g_final=None):
    t, d = x2d.shape
    ff = w_up.shape[1]
    tm = min(MLP_ROWS, t)
    has_proj = attn is not None
    has_final = g_final is not None
    row = pl.BlockSpec((tm, d), lambda i: (i, 0))
    args, specs = [x2d], [row]
    if has_proj:
        args += [attn, w_o.astype(BF16)]
        specs += [row, _resident((d, d))]
    args += [g.reshape(1, d), w_up.astype(BF16), w_down.astype(BF16)]
    specs += [_resident((1, d)), _resident((d, ff)), _resident((ff, d))]
    if has_final:
        args.append(g_final.reshape(1, d))
        specs.append(_resident((1, d)))
    return pl.pallas_call(
        functools.partial(_mlp_kernel, has_proj=has_proj, has_final=has_final,
                          ff_chunk=min(FF_CHUNK, ff)),
        grid=(t // tm,),
        in_specs=specs,
        out_specs=row,
        out_shape=jax.ShapeDtypeStruct((t, d), F32),
        compiler_params=_params(1),
        name="mlp_proj" if has_proj else "mlp",
    )(*args)


def _mlp_qkv_kernel(x_ref, g_ref, wup_ref, wdown_ref, ga_ref, wqkv_ref, gk_ref,
                    cos_ref, sin_ref, o_ref, q_ref, k_ref, v_ref, raw_ref, *, ff_chunk):
    i = pl.program_id(0)
    slot = i % 2
    n_q = q_ref.shape[1]

    @pl.when(i == 0)
    def _():
        raw_ref[1] = jnp.zeros(raw_ref.shape[1:], F32)

    _store_kv_heads(raw_ref.at[1 - slot], gk_ref, cos_ref, sin_ref, k_ref, v_ref)
    acc = _mlp_block(x_ref[...], g_ref, wup_ref, wdown_ref, ff_chunk)
    o_ref[...] = acc
    xn = _rms(acc, ga_ref[...]).astype(BF16)
    qkv = jnp.dot(xn, wqkv_ref[...], preferred_element_type=F32)
    q_ref[...] = qkv[:, :n_q]
    raw_ref[slot] = qkv[:, n_q:]


def _qk_gains(gq, gk):
    order = _head_lane_order()
    q_scale = HEAD_DIM ** -0.5 * np.log2(np.e)
    return (gq[order] * q_scale).reshape(1, HEAD_DIM), gk[order].reshape(1, HEAD_DIM)


def _mlp_qkv_layer(x, g, w_up, w_down, g_attn, w_qkv, gk):
    b, s, d = x.shape
    ff = w_up.shape[1]
    tm = min(MLP_ROWS, s)
    per_b = s // tm
    nt = b * per_b
    cos, sin = _rope_tables(s)
    qkv_dim = w_qkv.shape[1]
    n_q = N_HEADS * HEAD_DIM
    n_qk = (N_HEADS + N_KV_HEADS) * HEAD_DIM
    heads = np.arange(N_HEADS + N_KV_HEADS)[:, None] * HEAD_DIM
    cols = np.concatenate([(heads + _head_lane_order()[None, :]).ravel(), np.arange(n_qk, qkv_dim)])
    w = w_qkv[:, cols].astype(BF16)
    cur = lambda i: jnp.minimum(i, nt - 1)
    prev = lambda i: jnp.maximum(i - 1, 0)
    row = lambda width: pl.BlockSpec((tm, width), lambda i: (cur(i), 0))
    head = lambda width: pl.BlockSpec(
        (1, N_KV_HEADS, tm, width), lambda i: (prev(i) // per_b, 0, prev(i) % per_b, 0))
    tab = pl.BlockSpec((tm, HEAD_DIM), lambda i: (prev(i) % per_b, 0))
    x2, q, k, v = pl.pallas_call(
        functools.partial(_mlp_qkv_kernel, ff_chunk=min(FF_CHUNK, ff)),
        grid=(nt + 1,),
        in_specs=[
            row(d), _resident((1, d)), _resident((d, ff)), _resident((ff, d)),
            _resident((1, d)), _resident((d, qkv_dim)), _resident((1, HEAD_DIM)), tab, tab,
        ],
        out_specs=[row(d), row(n_q), head(HEAD_DIM), head(2 * HEAD_DIM)],
        out_shape=[
            jax.ShapeDtypeStruct((b * s, d), F32),
            jax.ShapeDtypeStruct((b * s, n_q), F32),
            jax.ShapeDtypeStruct((b, N_KV_HEADS, s, HEAD_DIM), BF16),
            jax.ShapeDtypeStruct((b, N_KV_HEADS, s, 2 * HEAD_DIM), BF16),
        ],
        scratch_shapes=[pltpu.VMEM((2, tm, qkv_dim - n_q), F32)],
        compiler_params=_params(1),
        name="mlp_qkv",
    )(x.reshape(b * s, d), g.reshape(1, d), w_up.astype(BF16), w_down.astype(BF16),
      g_attn.reshape(1, d), w, gk, jnp.asarray(cos), jnp.asarray(sin))
    return x2.reshape(b, s, d), q.reshape(b, s, n_q), k, v


def _attn_kernel(q_ref, gq_ref, cos_ref, sin_ref, k_ref, v_ref, o_ref, s0, s1, p0, p1, *, tq, nq):
    s_buf, p_buf = (s0, s1), (p0, p1)
    seq = k_ref.shape[2]

    def scores(row, h, slot):
        q = _rms(q_ref[0, pl.ds(row, tq), h * HEAD_DIM:(h + 1) * HEAD_DIM], gq_ref[...])
        q = _rope(q, cos_ref[pl.ds(row, tq), :], sin_ref[pl.ds(row, tq), :]).astype(BF16)
        s_buf[slot][:, :seq] = lax.dot_general(q, k_ref[0, 0], (((1,), (1,)), ((), ())),
                                               preferred_element_type=F32)

    def softmax(slot):
        sb, pb = s_buf[slot], p_buf[slot]
        m = sb[:, 0:LANES]
        for c in range(1, seq // LANES):
            m = jnp.maximum(m, sb[:, c * LANES:(c + 1) * LANES])
        mb = jnp.broadcast_to(jnp.max(m, axis=-1, keepdims=True), (tq, LANES))
        for c in range(seq // LANES):
            p = jnp.exp2(sb[:, c * LANES:(c + 1) * LANES] - mb)
            pb[:, c * LANES:(c + 1) * LANES] = p.astype(BF16)

    def weighted_values(row, h, slot):
        o = jnp.dot(p_buf[slot][:, :seq], v_ref[0, 0], preferred_element_type=F32)
        o = o[:, :HEAD_DIM] / o[:, HEAD_DIM:HEAD_DIM + 1]
        o_ref[0, pl.ds(row, tq), h * HEAD_DIM:(h + 1) * HEAD_DIM] = o.astype(BF16)

    p1[...] = jnp.ones_like(p1)
    scores(0, 0, 0)

    def tile_ticks(qi):
        row = pl.multiple_of(qi * tq, tq)
        row_next = pl.multiple_of(jnp.minimum(qi + 1, nq - 1) * tq, tq)
        row_prev = pl.multiple_of(jnp.maximum(qi - 1, 0) * tq, tq)
        for h in range(KV_GROUP):
            slot = h % 2
            if h + 1 < KV_GROUP:
                scores(row, h + 1, 1 - slot)
            else:
                scores(row_next, 0, 1 - slot)
            softmax(slot)
            if h > 0:
                weighted_values(row, h - 1, 1 - slot)
            else:
                weighted_values(row_prev, KV_GROUP - 1, 1 - slot)

    def body(qj, carry):
        for u in range(TILES_PER_TRIP):
            tile_ticks(qj * TILES_PER_TRIP + u)
        return carry

    lax.fori_loop(0, nq // TILES_PER_TRIP, body, 0)
    weighted_values((nq - 1) * tq, KV_GROUP - 1, 1)


def _attention(q, gq, k, v):
    b, s, _ = q.shape
    tq = min(ATTN_ROWS, s)
    nq = s // tq
    assert s % tq == 0 and nq % TILES_PER_TRIP == 0
    cos, sin = _rope_tables(s)
    whole = lambda width: pl.BlockSpec((1, 1, s, width), lambda i, g: (i, g, 0, 0))
    group = pl.BlockSpec((1, s, KV_GROUP * HEAD_DIM), lambda i, g: (i, 0, g))
    return pl.pallas_call(
        functools.partial(_attn_kernel, tq=tq, nq=nq),
        grid=(b, N_KV_HEADS),
        in_specs=[group, _resident((1, HEAD_DIM)), _resident((s, HEAD_DIM)), _resident((s, HEAD_DIM)),
                  whole(HEAD_DIM), whole(2 * HEAD_DIM)],
        out_specs=group,
        out_shape=jax.ShapeDtypeStruct((b, s, N_HEADS * HEAD_DIM), BF16),
        scratch_shapes=[pltpu.VMEM((tq, s + LANES), F32), pltpu.VMEM((tq, s + LANES), F32),
                        pltpu.VMEM((tq, s + LANES), BF16), pltpu.VMEM((tq, s + LANES), BF16)],
        compiler_params=_params(2),
        name="attention",
    )(q, gq, jnp.asarray(cos), jnp.asarray(sin), k, v)


def _trunk(x, fourier_norm, fourier_w_out, attn_norm, attn_w_qkv, attn_q_norm, attn_k_norm,
           attn_w_o, mlp_norm, mlp_w_up, mlp_w_down, final_norm):
    b, s, d = x.shape
    depth = mlp_norm.shape[0]
    qkv = None
    for i in range(depth):
        j = i // 2
        g_final = final_norm if i == depth - 1 else None
        if i % 2 == 0:
            x = _fourier_layer(x, fourier_norm[j], fourier_w_out[j])
            if i + 1 < depth:
                gq, gk = _qk_gains(attn_q_norm[j], attn_k_norm[j])
                x, q, k, v = _mlp_qkv_layer(x, mlp_norm[i], mlp_w_up[i], mlp_w_down[i], attn_norm[j],
                                            attn_w_qkv[j], gk)
                qkv = (q, gq, k, v)
            else:
                x = _mlp_layer(x.reshape(b * s, d), mlp_norm[i], mlp_w_up[i], mlp_w_down[i],
                               g_final=g_final).reshape(b, s, d)
        else:
            a = _attention(*qkv)
            x = _mlp_layer(x.reshape(b * s, d), mlp_norm[i], mlp_w_up[i], mlp_w_down[i],
                           attn=a.reshape(b * s, d), w_o=attn_w_o[j],
                           g_final=g_final).reshape(b, s, d)
    return x


def kernel(x_prompt, x_sample, fourier_norm, fourier_w_out, attn_norm, attn_w_qkv, attn_q_norm,
           attn_k_norm, attn_w_o, mlp_norm, mlp_w_up, mlp_w_down, final_norm):
    weights = (fourier_norm, fourier_w_out, attn_norm, attn_w_qkv, attn_q_norm, attn_k_norm,
               attn_w_o, mlp_norm, mlp_w_up, mlp_w_down, final_norm)
    return (_trunk(x_prompt, *weights), _trunk(x_sample, *weights))
```

```python
import functools

import numpy as np
import jax
import jax.numpy as jnp
from jax import lax
from jax.experimental import pallas as pl
from jax.experimental.pallas import tpu as pltpu

F32 = jnp.float32
BF16 = jnp.bfloat16

N_FGROUPS = 8
HEAD_DIM = 128
N_HEADS = 8
N_KV_HEADS = 2
KV_GROUP = N_HEADS // N_KV_HEADS
AXIS_DIM = HEAD_DIM // 2
ROPE_THETA = 10000.0
GRID_W = 64
EPS = 1e-6

V7X_VMEM_BYTES = 64 * 1024 * 1024
VMEM_LIMIT_BYTES = V7X_VMEM_BYTES - 8 * 1024 * 1024
LANES = 128

RADIX = 8
FOURIER_ROWS = 512
MLP_ROWS = 512
FF_CHUNK = 1024
ATTN_ROWS = 256
TILES_PER_TRIP = 4


def _params(n_grid_dims):
    return pltpu.CompilerParams(
        dimension_semantics=("arbitrary",) * n_grid_dims,
        vmem_limit_bytes=VMEM_LIMIT_BYTES,
    )


def _rms(x, g):
    return x * lax.rsqrt(jnp.mean(x * x, axis=-1, keepdims=True) + EPS) * g


def _resident(shape):
    nd = len(shape)
    return pl.BlockSpec(shape, lambda *_: (0,) * nd, pipeline_mode=pl.Buffered(1))


def _fourier_tables(s, d, tk):
    n = s // RADIX
    k = np.arange(n)
    ang = 2.0 * np.pi * ((k[:, None] * k[None, :]) % n) / n
    c = np.cos(ang).reshape(n // tk, tk, n)
    sn = np.sin(ang).reshape(n // tk, tk, n)
    dft = np.concatenate([c, -sn], axis=1).astype(np.float32)
    tw = np.zeros((n, 2 * RADIX), np.float32)
    for j in range(1, RADIX):
        th = 2.0 * np.pi * ((k * j) % s) / s
        tw[:, 2 * (j - 1)] = np.cos(th)
        tw[:, 2 * (j - 1) + 1] = -np.sin(th)
    fg = d // N_FGROUPS
    cc = np.arange(fg)
    cang = 2.0 * np.pi * ((cc[:, None] * cc[None, :]) % fg) / fg
    scale = 1.0 / np.sqrt(float(s) * fg)
    chan = np.concatenate([np.cos(cang), np.sin(cang)], axis=0) * scale
    return dft, tw, chan.astype(np.float32)


def _times_root(z, m, r):
    re, im = z
    m %= r
    if m == 0:
        return re, im
    if 4 * m == r:
        return im, -re
    if 2 * m == r:
        return -re, -im
    if 4 * m == 3 * r:
        return -im, re
    wr, wi = float(np.cos(2.0 * np.pi * m / r)), float(-np.sin(2.0 * np.pi * m / r))
    if 8 * m % r == 0:
        a = abs(wr)
        sr, si = np.sign(wr), np.sign(wi)
        return (sr * re - si * im) * a, (si * re + sr * im) * a
    return re * wr - im * wi, re * wi + im * wr


def _small_dft(zs):
    r = len(zs)
    if r == 1:
        return zs
    even, odd = _small_dft(zs[0::2]), _small_dft(zs[1::2])
    out = [None] * r
    for q in range(r // 2):
        tr, ti = _times_root(odd[q], q, r)
        out[q] = (even[q][0] + tr, even[q][1] + ti)
        out[q + r // 2] = (even[q][0] - tr, even[q][1] - ti)
    return out


def _normalise_rows(xt_ref, g_ref, cols_ref, xr_ref, j, d):
    rows = xt_ref.shape[1] // RADIX
    xn = _rms(xt_ref[0], g_ref[...])
    for c in range(d // LANES):
        cols_ref[c] = xn[:, c * LANES:(c + 1) * LANES]
    r0 = pl.multiple_of(j * rows, rows)
    for jr in range(RADIX):
        for c in range(d // LANES):
            xr_ref[pl.ds(r0, rows), jr * d + c * LANES:jr * d + (c + 1) * LANES] = (
                cols_ref[c, pl.ds(jr, rows, stride=RADIX), :].astype(BF16))


def _fourier_tile(dft_ref, xr_ref, tw_ref, chan_ref, w_ref, x_ref, o_ref, tk, d):
    e = jnp.dot(dft_ref[0], xr_ref[...], preferred_element_type=F32)
    tw = tw_ref[...]
    zs = [(e[:tk, 0:d], e[tk:, 0:d])]
    for j in range(1, RADIX):
        er, ei = e[:tk, j * d:(j + 1) * d], e[tk:, j * d:(j + 1) * d]
        tr = tw[:, 2 * (j - 1):2 * (j - 1) + 1]
        ti = tw[:, 2 * (j - 1) + 1:2 * (j - 1) + 2]
        zs.append((er * tr - ei * ti, er * ti + ei * tr))
    us = _small_dft(zs)
    ur = jnp.concatenate([u[0] for u in us], axis=0).astype(BF16)
    ui = jnp.concatenate([u[1] for u in us], axis=0).astype(BF16)
    fg = d // N_FGROUPS
    chan = chan_ref[...]
    f = []
    for g in range(N_FGROUPS):
        lhs = jnp.concatenate([ur[:, g * fg:(g + 1) * fg], ui[:, g * fg:(g + 1) * fg]], axis=1)
        f.append(jnp.dot(lhs, chan, preferred_element_type=F32))
    f = jnp.concatenate(f, axis=1).astype(BF16)
    y = jnp.dot(f, w_ref[...], preferred_element_type=F32)
    o_ref[0] = x_ref[0] + y.reshape(RADIX, tk, d)


def _fourier_kernel(dft_ref, xt_ref, g_ref, tw_ref, chan_ref, w_ref, x_ref, o_ref,
                    xr0, xr1, cols_ref, *, tk, d):
    i, j = pl.program_id(0), pl.program_id(1)

    def step(xr_fill, xr_mix):
        _fourier_tile(dft_ref, xr_mix, tw_ref, chan_ref, w_ref, x_ref, o_ref, tk, d)
        _normalise_rows(xt_ref, g_ref, cols_ref, xr_fill, j, d)

    @pl.when(i == 0)
    def _():
        _normalise_rows(xt_ref, g_ref, cols_ref, xr0, j, d)

    @pl.when((i > 0) & (i % 2 == 0))
    def _():
        step(xr0, xr1)

    @pl.when(i % 2 == 1)
    def _():
        step(xr1, xr0)


def _fourier_layer(x, g, w_out):
    b, s, d = x.shape
    n = s // RADIX
    tk = min(FOURIER_ROWS // RADIX, n)
    nk = n // tk
    tokens = s // nk
    dft, tw, chan = _fourier_tables(s, d, tk)
    dft = jnp.asarray(dft).astype(BF16)
    chan = jnp.asarray(chan).astype(BF16)
    x4 = x.reshape(b, RADIX, n, d)
    fg = d // N_FGROUPS
    fill = lambda i: jnp.minimum(i, b - 1)
    mix = lambda i: jnp.maximum(i - 1, 0)
    blocks = pl.BlockSpec((1, RADIX, tk, d), lambda i, j: (mix(i), 0, jnp.where(i == 0, 0, j), 0))
    out = pl.pallas_call(
        functools.partial(_fourier_kernel, tk=tk, d=d),
        grid=(b + 1, nk),
        in_specs=[
            pl.BlockSpec((1, 2 * tk, n), lambda i, j: (j, 0, 0)),
            pl.BlockSpec((1, tokens, d), lambda i, j: (fill(i), j, 0)),
            _resident((1, d)),
            pl.BlockSpec((tk, 2 * RADIX), lambda i, j: (j, 0)),
            _resident((2 * fg, fg)),
            _resident((d, d)),
            blocks,
        ],
        out_specs=blocks,
        out_shape=jax.ShapeDtypeStruct((b, RADIX, n, d), F32),
        scratch_shapes=[pltpu.VMEM((n, RADIX * d), BF16), pltpu.VMEM((n, RADIX * d), BF16),
                        pltpu.VMEM((d // LANES, tokens, LANES), F32)],
        compiler_params=_params(2),
        name="fourier",
    )(dft, x, g.reshape(1, d), jnp.asarray(tw), chan, w_out.astype(BF16), x4)
    return out.reshape(b, s, d)


def _head_lane_order():
    quarter = AXIS_DIM // 2
    blocks = np.arange(HEAD_DIM).reshape(4, quarter)
    return np.concatenate([blocks[0], blocks[2], blocks[1], blocks[3]])


def _rope_tables(s):
    pos = np.arange(s)
    inv = ROPE_THETA ** (-np.arange(0, AXIS_DIM, 2, dtype=np.float64) / AXIS_DIM)
    ang = np.concatenate([(pos // GRID_W)[:, None] * inv[None, :],
                          (pos % GRID_W)[:, None] * inv[None, :]], axis=1)
    cos = np.concatenate([np.cos(ang), np.cos(ang)], axis=1)
    sin = np.concatenate([-np.sin(ang), np.sin(ang)], axis=1)
    return cos.astype(np.float32), sin.astype(np.float32)


def _rope(x, cos, sin):
    return x * cos + pltpu.roll(x, HEAD_DIM // 2, 1) * sin


def _store_kv_heads(raw, gk_ref, cos_ref, sin_ref, k_ref, v_ref):
    cos, sin = cos_ref[...], sin_ref[...]
    lane = lax.broadcasted_iota(jnp.int32, cos.shape, 1)
    ones_col = jnp.where(lane == 0, 1.0, 0.0).astype(BF16)
    for h in range(N_KV_HEADS):
        kh = _rms(raw[:, h * HEAD_DIM:(h + 1) * HEAD_DIM], gk_ref[...])
        k_ref[0, h] = _rope(kh, cos, sin).astype(BF16)
        off = (N_KV_HEADS + h) * HEAD_DIM
        v_ref[0, h, :, :HEAD_DIM] = raw[:, off:off + HEAD_DIM].astype(BF16)
        v_ref[0, h, :, HEAD_DIM:] = ones_col


def _mlp_block(x, g_ref, wup_ref, wdown_ref, ff_chunk):
    xn = _rms(x, g_ref[...]).astype(BF16)
    acc = x
    for c in range(wup_ref.shape[1] // ff_chunk):
        h = jnp.dot(xn, wup_ref[:, c * ff_chunk:(c + 1) * ff_chunk], preferred_element_type=F32)
        h = jnp.maximum(h, 0.0)
        h = (h * h).astype(BF16)
        acc = acc + jnp.dot(h, wdown_ref[c * ff_chunk:(c + 1) * ff_chunk, :],
                            preferred_element_type=F32)
    return acc


def _mlp_kernel(*refs, has_proj, has_final, ff_chunk):
    refs = list(refs)
    x_ref = refs.pop(0)
    if has_proj:
        a_ref, wo_ref = refs.pop(0), refs.pop(0)
    g_ref, wup_ref, wdown_ref = refs.pop(0), refs.pop(0), refs.pop(0)
    if has_final:
        gf_ref = refs.pop(0)
    o_ref = refs.pop(0)

    x = x_ref[...]
    if has_proj:
        x = x + jnp.dot(a_ref[...], wo_ref[...], preferred_element_type=F32)
    acc = _mlp_block(x, g_ref, wup_ref, wdown_ref, ff_chunk)
    if has_final:
        acc = _rms(acc, gf_ref[...])
    o_ref[...] = acc


def _mlp_layer(x2d, g, w_up, w_down, attn=None, w_o=None, g_final=None):
    t, d = x2d.shape
    ff = w_up.shape[1]
    tm = min(MLP_ROWS, t)
    has_proj = attn is not None
    has_final = g_final is not None
    row = pl.BlockSpec((tm, d), lambda i: (i, 0))
    args, specs = [x2d], [row]
    if has_proj:
        args += [attn, w_o.astype(BF16)]
        specs += [row, _resident((d, d))]
    args += [g.reshape(1, d), w_up.astype(BF16), w_down.astype(BF16)]
    specs += [_resident((1, d)), _resident((d, ff)), _resident((ff, d))]
    if has_final:
        args.append(g_final.reshape(1, d))
        specs.append(_resident((1, d)))
    return pl.pallas_call(
        functools.partial(_mlp_kernel, has_proj=has_proj, has_final=has_final,
                          ff_chunk=min(FF_CHUNK, ff)),
        grid=(t // tm,),
        in_specs=specs,
        out_specs=row,
        out_shape=jax.ShapeDtypeStruct((t, d), F32),
        compiler_params=_params(1),
        name="mlp_proj" if has_proj else "mlp",
    )(*args)


def _mlp_qkv_kernel(x_ref, g_ref, wup_ref, wdown_ref, ga_ref, wqkv_ref, gk_ref,
                    cos_ref, sin_ref, o_ref, q_ref, k_ref, v_ref, raw_ref, *, ff_chunk):
    i = pl.program_id(0)
    slot = i % 2
    n_q = q_ref.shape[1]

    @pl.when(i == 0)
    def _():
        raw_ref[1] = jnp.zeros(raw_ref.shape[1:], F32)

    _store_kv_heads(raw_ref.at[1 - slot], gk_ref, cos_ref, sin_ref, k_ref, v_ref)
    acc = _mlp_block(x_ref[...], g_ref, wup_ref, wdown_ref, ff_chunk)
    o_ref[...] = acc
    xn = _rms(acc, ga_ref[...]).astype(BF16)
    qkv = jnp.dot(xn, wqkv_ref[...], preferred_element_type=F32)
    q_ref[...] = qkv[:, :n_q]
    raw_ref[slot] = qkv[:, n_q:]


def _qk_gains(gq, gk):
    order = _head_lane_order()
    q_scale = HEAD_DIM ** -0.5 * np.log2(np.e)
    return (gq[order] * q_scale).reshape(1, HEAD_DIM), gk[order].reshape(1, HEAD_DIM)


def _mlp_qkv_layer(x, g, w_up, w_down, g_attn, w_qkv, gk):
    b, s, d = x.shape
    ff = w_up.shape[1]
    tm = min(MLP_ROWS, s)
    per_b = s // tm
    nt = b * per_b
    cos, sin = _rope_tables(s)
    qkv_dim = w_qkv.shape[1]
    n_q = N_HEADS * HEAD_DIM
    n_qk = (N_HEADS + N_KV_HEADS) * HEAD_DIM
    heads = np.arange(N_HEADS + N_KV_HEADS)[:, None] * HEAD_DIM
    cols = np.concatenate([(heads + _head_lane_order()[None, :]).ravel(), np.arange(n_qk, qkv_dim)])
    w = w_qkv[:, cols].astype(BF16)
    cur = lambda i: jnp.minimum(i, nt - 1)
    prev = lambda i: jnp.maximum(i - 1, 0)
    row = lambda width: pl.BlockSpec((tm, width), lambda i: (cur(i), 0))
    head = lambda width: pl.BlockSpec(
        (1, N_KV_HEADS, tm, width), lambda i: (prev(i) // per_b, 0, prev(i) % per_b, 0))
    tab = pl.BlockSpec((tm, HEAD_DIM), lambda i: (prev(i) % per_b, 0))
    x2, q, k, v = pl.pallas_call(
        functools.partial(_mlp_qkv_kernel, ff_chunk=min(FF_CHUNK, ff)),
        grid=(nt + 1,),
        in_specs=[
            row(d), _resident((1, d)), _resident((d, ff)), _resident((ff, d)),
            _resident((1, d)), _resident((d, qkv_dim)), _resident((1, HEAD_DIM)), tab, tab,
        ],
        out_specs=[row(d), row(n_q), head(HEAD_DIM), head(2 * HEAD_DIM)],
        out_shape=[
            jax.ShapeDtypeStruct((b * s, d), F32),
            jax.ShapeDtypeStruct((b * s, n_q), F32),
            jax.ShapeDtypeStruct((b, N_KV_HEADS, s, HEAD_DIM), BF16),
            jax.ShapeDtypeStruct((b, N_KV_HEADS, s, 2 * HEAD_DIM), BF16),
        ],
        scratch_shapes=[pltpu.VMEM((2, tm, qkv_dim - n_q), F32)],
        compiler_params=_params(1),
        name="mlp_qkv",
    )(x.reshape(b * s, d), g.reshape(1, d), w_up.astype(BF16), w_down.astype(BF16),
      g_attn.reshape(1, d), w, gk, jnp.asarray(cos), jnp.asarray(sin))
    return x2.reshape(b, s, d), q.reshape(b, s, n_q), k, v


def _attn_kernel(q_ref, gq_ref, cos_ref, sin_ref, k_ref, v_ref, o_ref, s0, s1, p0, p1, *, tq, nq):
    s_buf, p_buf = (s0, s1), (p0, p1)
    seq = k_ref.shape[2]

    def scores(row, h, slot):
        q = _rms(q_ref[0, pl.ds(row, tq), h * HEAD_DIM:(h + 1) * HEAD_DIM], gq_ref[...])
        q = _rope(q, cos_ref[pl.ds(row, tq), :], sin_ref[pl.ds(row, tq), :]).astype(BF16)
        s_buf[slot][:, :seq] = lax.dot_general(q, k_ref[0, 0], (((1,), (1,)), ((), ())),
                                               preferred_element_type=F32)

    def softmax(slot):
        sb, pb = s_buf[slot], p_buf[slot]
        m = sb[:, 0:LANES]
        for c in range(1, seq // LANES):
            m = jnp.maximum(m, sb[:, c * LANES:(c + 1) * LANES])
        mb = jnp.broadcast_to(jnp.max(m, axis=-1, keepdims=True), (tq, LANES))
        for c in range(seq // LANES):
            p = jnp.exp2(sb[:, c * LANES:(c + 1) * LANES] - mb)
            pb[:, c * LANES:(c + 1) * LANES] = p.astype(BF16)

    def weighted_values(row, h, slot):
        o = jnp.dot(p_buf[slot][:, :seq], v_ref[0, 0], preferred_element_type=F32)
        o = o[:, :HEAD_DIM] / o[:, HEAD_DIM:HEAD_DIM + 1]
        o_ref[0, pl.ds(row, tq), h * HEAD_DIM:(h + 1) * HEAD_DIM] = o.astype(BF16)

    p1[...] = jnp.ones_like(p1)
    scores(0, 0, 0)

    def tile_ticks(qi):
        row = pl.multiple_of(qi * tq, tq)
        row_next = pl.multiple_of(jnp.minimum(qi + 1, nq - 1) * tq, tq)
        row_prev = pl.multiple_of(jnp.maximum(qi - 1, 0) * tq, tq)
        for h in range(KV_GROUP):
            slot = h % 2
            if h + 1 < KV_GROUP:
                scores(row, h + 1, 1 - slot)
            else:
                scores(row_next, 0, 1 - slot)
            softmax(slot)
            if h > 0:
                weighted_values(row, h - 1, 1 - slot)
            else:
                weighted_values(row_prev, KV_GROUP - 1, 1 - slot)

    def body(qj, carry):
        for u in range(TILES_PER_TRIP):
            tile_ticks(qj * TILES_PER_TRIP + u)
        return carry

    lax.fori_loop(0, nq // TILES_PER_TRIP, body, 0)
    weighted_values((nq - 1) * tq, KV_GROUP - 1, 1)


def _attention(q, gq, k, v):
    b, s, _ = q.shape
    tq = min(ATTN_ROWS, s)
    nq = s // tq
    assert s % tq == 0 and nq % TILES_PER_TRIP == 0
    cos, sin = _rope_tables(s)
    whole = lambda width: pl.BlockSpec((1, 1, s, width), lambda i, g: (i, g, 0, 0))
    group = pl.BlockSpec((1, s, KV_GROUP * HEAD_DIM), lambda i, g: (i, 0, g))
    return pl.pallas_call(
        functools.partial(_attn_kernel, tq=tq, nq=nq),
        grid=(b, N_KV_HEADS),
        in_specs=[group, _resident((1, HEAD_DIM)), _resident((s, HEAD_DIM)), _resident((s, HEAD_DIM)),
                  whole(HEAD_DIM), whole(2 * HEAD_DIM)],
        out_specs=group,
        out_shape=jax.ShapeDtypeStruct((b, s, N_HEADS * HEAD_DIM), BF16),
        scratch_shapes=[pltpu.VMEM((tq, s + LANES), F32), pltpu.VMEM((tq, s + LANES), F32),
                        pltpu.VMEM((tq, s + LANES), BF16), pltpu.VMEM((tq, s + LANES), BF16)],
        compiler_params=_params(2),
        name="attention",
    )(q, gq, jnp.asarray(cos), jnp.asarray(sin), k, v)


def _trunk(x, fourier_norm, fourier_w_out, attn_norm, attn_w_qkv, attn_q_norm, attn_k_norm,
           attn_w_o, mlp_norm, mlp_w_up, mlp_w_down, final_norm):
    b, s, d = x.shape
    depth = mlp_norm.shape[0]
    qkv = None
    for i in range(depth):
        j = i // 2
        g_final = final_norm if i == depth - 1 else None
        if i % 2 == 0:
            x = _fourier_layer(x, fourier_norm[j], fourier_w_out[j])
            if i + 1 < depth:
                gq, gk = _qk_gains(attn_q_norm[j], attn_k_norm[j])
                x, q, k, v = _mlp_qkv_layer(x, mlp_norm[i], mlp_w_up[i], mlp_w_down[i], attn_norm[j],
                                            attn_w_qkv[j], gk)
                qkv = (q, gq, k, v)
            else:
                x = _mlp_layer(x.reshape(b * s, d), mlp_norm[i], mlp_w_up[i], mlp_w_down[i],
                               g_final=g_final).reshape(b, s, d)
        else:
            a = _attention(*qkv)
            x = _mlp_layer(x.reshape(b * s, d), mlp_norm[i], mlp_w_up[i], mlp_w_down[i],
                           attn=a.reshape(b * s, d), w_o=attn_w_o[j],
                           g_final=g_final).reshape(b, s, d)
    return x


def kernel(x_prompt, x_sample, fourier_norm, fourier_w_out, attn_norm, attn_w_qkv, attn_q_norm,
           attn_k_norm, attn_w_o, mlp_norm, mlp_w_up, mlp_w_down, final_norm):
    weights = (fourier_norm, fourier_w_out, attn_norm, attn_w_qkv, attn_q_norm, attn_k_norm,
               attn_w_o, mlp_norm, mlp_w_up, mlp_w_down, final_norm)
    return (_trunk(x_prompt, *weights), _trunk(x_sample, *weights))
```

```python
import functools

import numpy as np
import jax
import jax.numpy as jnp
from jax import lax
from jax.experimental import pallas as pl
from jax.experimental.pallas import tpu as pltpu

F32 = jnp.float32
BF16 = jnp.bfloat16

N_FGROUPS = 8
HEAD_DIM = 128
N_HEADS = 8
N_KV_HEADS = 2
KV_GROUP = N_HEADS // N_KV_HEADS
AXIS_DIM = HEAD_DIM // 2
ROPE_THETA = 10000.0
GRID_W = 64
EPS = 1e-6

V7X_VMEM_BYTES = 64 * 1024 * 1024
VMEM_LIMIT_BYTES = V7X_VMEM_BYTES - 8 * 1024 * 1024
LANES = 128

RADIX = 8
FOURIER_ROWS = 512
MLP_ROWS = 512
FF_CHUNK = 1024
ATTN_ROWS = 128
TILES_PER_TRIP = 8


def _params(n_grid_dims):
    return pltpu.CompilerParams(
        dimension_semantics=("arbitrary",) * n_grid_dims,
        vmem_limit_bytes=VMEM_LIMIT_BYTES,
    )


def _rms(x, g):
    return x * lax.rsqrt(jnp.mean(x * x, axis=-1, keepdims=True) + EPS) * g


def _resident(shape):
    nd = len(shape)
    return pl.BlockSpec(shape, lambda *_: (0,) * nd, pipeline_mode=pl.Buffered(1))


def _fourier_tables(s, d, tk):
    n = s // RADIX
    k = np.arange(n)
    ang = 2.0 * np.pi * ((k[:, None] * k[None, :]) % n) / n
    c = np.cos(ang).reshape(n // tk, tk, n)
    sn = np.sin(ang).reshape(n // tk, tk, n)
    dft = np.concatenate([c, -sn], axis=1).astype(np.float32)
    tw = np.zeros((n, 2 * RADIX), np.float32)
    for j in range(1, RADIX):
        th = 2.0 * np.pi * ((k * j) % s) / s
        tw[:, 2 * (j - 1)] = np.cos(th)
        tw[:, 2 * (j - 1) + 1] = -np.sin(th)
    fg = d // N_FGROUPS
    cc = np.arange(fg)
    cang = 2.0 * np.pi * ((cc[:, None] * cc[None, :]) % fg) / fg
    scale = 1.0 / np.sqrt(float(s) * fg)
    chan = np.concatenate([np.cos(cang), np.sin(cang)], axis=0) * scale
    return dft, tw, chan.astype(np.float32)


def _times_root(z, m, r):
    re, im = z
    m %= r
    if m == 0:
        return re, im
    if 4 * m == r:
        return im, -re
    if 2 * m == r:
        return -re, -im
    if 4 * m == 3 * r:
        return -im, re
    wr, wi = float(np.cos(2.0 * np.pi * m / r)), float(-np.sin(2.0 * np.pi * m / r))
    if 8 * m % r == 0:
        a = abs(wr)
        sr, si = np.sign(wr), np.sign(wi)
        return (sr * re - si * im) * a, (si * re + sr * im) * a
    return re * wr - im * wi, re * wi + im * wr


def _small_dft(zs):
    r = len(zs)
    if r == 1:
        return zs
    even, odd = _small_dft(zs[0::2]), _small_dft(zs[1::2])
    out = [None] * r
    for q in range(r // 2):
        tr, ti = _times_root(odd[q], q, r)
        out[q] = (even[q][0] + tr, even[q][1] + ti)
        out[q + r // 2] = (even[q][0] - tr, even[q][1] - ti)
    return out


def _normalise_rows(xt_ref, g_ref, cols_ref, xr_ref, j, d):
    rows = xt_ref.shape[1] // RADIX
    xn = _rms(xt_ref[0], g_ref[...])
    for c in range(d // LANES):
        cols_ref[c] = xn[:, c * LANES:(c + 1) * LANES]
    r0 = pl.multiple_of(j * rows, rows)
    for jr in range(RADIX):
        for c in range(d // LANES):
            xr_ref[pl.ds(r0, rows), jr * d + c * LANES:jr * d + (c + 1) * LANES] = (
                cols_ref[c, pl.ds(jr, rows, stride=RADIX), :].astype(BF16))


def _fourier_tile(dft_ref, xr_ref, tw_ref, chan_ref, w_ref, x_ref, o_ref, tk, d):
    e = jnp.dot(dft_ref[0], xr_ref[...], preferred_element_type=F32)
    tw = tw_ref[...]
    zs = [(e[:tk, 0:d], e[tk:, 0:d])]
    for j in range(1, RADIX):
        er, ei = e[:tk, j * d:(j + 1) * d], e[tk:, j * d:(j + 1) * d]
        tr = tw[:, 2 * (j - 1):2 * (j - 1) + 1]
        ti = tw[:, 2 * (j - 1) + 1:2 * (j - 1) + 2]
        zs.append((er * tr - ei * ti, er * ti + ei * tr))
    us = _small_dft(zs)
    ur = jnp.concatenate([u[0] for u in us], axis=0).astype(BF16)
    ui = jnp.concatenate([u[1] for u in us], axis=0).astype(BF16)
    fg = d // N_FGROUPS
    chan = chan_ref[...]
    f = []
    for g in range(N_FGROUPS):
        lhs = jnp.concatenate([ur[:, g * fg:(g + 1) * fg], ui[:, g * fg:(g + 1) * fg]], axis=1)
        f.append(jnp.dot(lhs, chan, preferred_element_type=F32))
    f = jnp.concatenate(f, axis=1).astype(BF16)
    y = jnp.dot(f, w_ref[...], preferred_element_type=F32)
    o_ref[0] = x_ref[0] + y.reshape(RADIX, tk, d)


def _fourier_kernel(dft_ref, xt_ref, g_ref, tw_ref, chan_ref, w_ref, x_ref, o_ref,
                    xr0, xr1, cols_ref, *, tk, d):
    i, j = pl.program_id(0), pl.program_id(1)

    def step(xr_fill, xr_mix):
        _fourier_tile(dft_ref, xr_mix, tw_ref, chan_ref, w_ref, x_ref, o_ref, tk, d)
        _normalise_rows(xt_ref, g_ref, cols_ref, xr_fill, j, d)

    @pl.when(i == 0)
    def _():
        _normalise_rows(xt_ref, g_ref, cols_ref, xr0, j, d)

    @pl.when((i > 0) & (i % 2 == 0))
    def _():
        step(xr0, xr1)

    @pl.when(i % 2 == 1)
    def _():
        step(xr1, xr0)


def _fourier_layer(x, g, w_out):
    b, s, d = x.shape
    n = s // RADIX
    tk = min(FOURIER_ROWS // RADIX, n)
    nk = n // tk
    tokens = s // nk
    dft, tw, chan = _fourier_tables(s, d, tk)
    dft = jnp.asarray(dft).astype(BF16)
    chan = jnp.asarray(chan).astype(BF16)
    x4 = x.reshape(b, RADIX, n, d)
    fg = d // N_FGROUPS
    fill = lambda i: jnp.minimum(i, b - 1)
    mix = lambda i: jnp.maximum(i - 1, 0)
    blocks = pl.BlockSpec((1, RADIX, tk, d), lambda i, j: (mix(i), 0, jnp.where(i == 0, 0, j), 0))
    out = pl.pallas_call(
        functools.partial(_fourier_kernel, tk=tk, d=d),
        grid=(b + 1, nk),
        in_specs=[
            pl.BlockSpec((1, 2 * tk, n), lambda i, j: (j, 0, 0)),
            pl.BlockSpec((1, tokens, d), lambda i, j: (fill(i), j, 0)),
            _resident((1, d)),
            pl.BlockSpec((tk, 2 * RADIX), lambda i, j: (j, 0)),
            _resident((2 * fg, fg)),
            _resident((d, d)),
            blocks,
        ],
        out_specs=blocks,
        out_shape=jax.ShapeDtypeStruct((b, RADIX, n, d), F32),
        scratch_shapes=[pltpu.VMEM((n, RADIX * d), BF16), pltpu.VMEM((n, RADIX * d), BF16),
                        pltpu.VMEM((d // LANES, tokens, LANES), F32)],
        compiler_params=_params(2),
        name="fourier",
    )(dft, x, g.reshape(1, d), jnp.asarray(tw), chan, w_out.astype(BF16), x4)
    return out.reshape(b, s, d)


def _head_lane_order():
    quarter = AXIS_DIM // 2
    blocks = np.arange(HEAD_DIM).reshape(4, quarter)
    return np.concatenate([blocks[0], blocks[2], blocks[1], blocks[3]])


def _rope_tables(s):
    pos = np.arange(s)
    inv = ROPE_THETA ** (-np.arange(0, AXIS_DIM, 2, dtype=np.float64) / AXIS_DIM)
    ang = np.concatenate([(pos // GRID_W)[:, None] * inv[None, :],
                          (pos % GRID_W)[:, None] * inv[None, :]], axis=1)
    cos = np.concatenate([np.cos(ang), np.cos(ang)], axis=1)
    sin = np.concatenate([-np.sin(ang), np.sin(ang)], axis=1)
    return cos.astype(np.float32), sin.astype(np.float32)


def _rope(x, cos, sin):
    return x * cos + pltpu.roll(x, HEAD_DIM // 2, 1) * sin


def _store_kv_heads(raw, gk_ref, cos_ref, sin_ref, k_ref, v_ref):
    cos, sin = cos_ref[...], sin_ref[...]
    lane = lax.broadcasted_iota(jnp.int32, cos.shape, 1)
    ones_col = jnp.where(lane == 0, 1.0, 0.0).astype(BF16)
    for h in range(N_KV_HEADS):
        kh = _rms(raw[:, h * HEAD_DIM:(h + 1) * HEAD_DIM], gk_ref[...])
        k_ref[0, h] = _rope(kh, cos, sin).astype(BF16)
        off = (N_KV_HEADS + h) * HEAD_DIM
        v_ref[0, h, :, :HEAD_DIM] = raw[:, off:off + HEAD_DIM].astype(BF16)
        v_ref[0, h, :, HEAD_DIM:] = ones_col


def _mlp_block(x, g_ref, wup_ref, wdown_ref, ff_chunk):
    xn = _rms(x, g_ref[...]).astype(BF16)
    acc = x
    for c in range(wup_ref.shape[1] // ff_chunk):
        h = jnp.dot(xn, wup_ref[:, c * ff_chunk:(c + 1) * ff_chunk], preferred_element_type=F32)
        h = jnp.maximum(h, 0.0)
        h = (h * h).astype(BF16)
        acc = acc + jnp.dot(h, wdown_ref[c * ff_chunk:(c + 1) * ff_chunk, :],
                            preferred_element_type=F32)
    return acc


def _mlp_kernel(*refs, has_proj, has_final, ff_chunk):
    refs = list(refs)
    x_ref = refs.pop(0)
    if has_proj:
        a_ref, wo_ref = refs.pop(0), refs.pop(0)
    g_ref, wup_ref, wdown_ref = refs.pop(0), refs.pop(0), refs.pop(0)
    if has_final:
        gf_ref = refs.pop(0)
    o_ref = refs.pop(0)

    x = x_ref[...]
    if has_proj:
        x = x + jnp.dot(a_ref[...], wo_ref[...], preferred_element_type=F32)
    acc = _mlp_block(x, g_ref, wup_ref, wdown_ref, ff_chunk)
    if has_final:
        acc = _rms(acc, gf_ref[...])
    o_ref[...] = acc


def _mlp_layer(x2d, g, w_up, w_down, attn=None, w_o=None, g_final=None):
    t, d = x2d.shape
    ff = w_up.shape[1]
    tm = min(MLP_ROWS, t)
    has_proj = attn is not None
    has_final = g_final is not None
    row = pl.BlockSpec((tm, d), lambda i: (i, 0))
    args, specs = [x2d], [row]
    if has_proj:
        args += [attn, w_o.astype(BF16)]
        specs += [row, _resident((d, d))]
    args += [g.reshape(1, d), w_up.astype(BF16), w_down.astype(BF16)]
    specs += [_resident((1, d)), _resident((d, ff)), _resident((ff, d))]
    if has_final:
        args.append(g_final.reshape(1, d))
        specs.append(_resident((1, d)))
    return pl.pallas_call(
        functools.partial(_mlp_kernel, has_proj=has_proj, has_final=has_final,
                          ff_chunk=min(FF_CHUNK, ff)),
        grid=(t // tm,),
        in_specs=specs,
        out_specs=row,
        out_shape=jax.ShapeDtypeStruct((t, d), F32),
        compiler_params=_params(1),
        name="mlp_proj" if has_proj else "mlp",
    )(*args)


def _mlp_qkv_kernel(x_ref, g_ref, wup_ref, wdown_ref, ga_ref, wqkv_ref, gk_ref,
                    cos_ref, sin_ref, o_ref, q_ref, k_ref, v_ref, raw_ref, *, ff_chunk):
    i = pl.program_id(0)
    slot = i % 2
    n_q = q_ref.shape[1]

    @pl.when(i == 0)
    def _():
        raw_ref[1] = jnp.zeros(raw_ref.shape[1:], F32)

    _store_kv_heads(raw_ref.at[1 - slot], gk_ref, cos_ref, sin_ref, k_ref, v_ref)
    acc = _mlp_block(x_ref[...], g_ref, wup_ref, wdown_ref, ff_chunk)
    o_ref[...] = acc
    xn = _rms(acc, ga_ref[...]).astype(BF16)
    qkv = jnp.dot(xn, wqkv_ref[...], preferred_element_type=F32)
    q_ref[...] = qkv[:, :n_q]
    raw_ref[slot] = qkv[:, n_q:]


def _qk_gains(gq, gk):
    order = _head_lane_order()
    q_scale = HEAD_DIM ** -0.5 * np.log2(np.e)
    return (gq[order] * q_scale).reshape(1, HEAD_DIM), gk[order].reshape(1, HEAD_DIM)


def _mlp_qkv_layer(x, g, w_up, w_down, g_attn, w_qkv, gk):
    b, s, d = x.shape
    ff = w_up.shape[1]
    tm = min(MLP_ROWS, s)
    per_b = s // tm
    nt = b * per_b
    cos, sin = _rope_tables(s)
    qkv_dim = w_qkv.shape[1]
    n_q = N_HEADS * HEAD_DIM
    n_qk = (N_HEADS + N_KV_HEADS) * HEAD_DIM
    heads = np.arange(N_HEADS + N_KV_HEADS)[:, None] * HEAD_DIM
    cols = np.concatenate([(heads + _head_lane_order()[None, :]).ravel(), np.arange(n_qk, qkv_dim)])
    w = w_qkv[:, cols].astype(BF16)
    cur = lambda i: jnp.minimum(i, nt - 1)
    prev = lambda i: jnp.maximum(i - 1, 0)
    row = lambda width: pl.BlockSpec((tm, width), lambda i: (cur(i), 0))
    head = lambda width: pl.BlockSpec(
        (1, N_KV_HEADS, tm, width), lambda i: (prev(i) // per_b, 0, prev(i) % per_b, 0))
    tab = pl.BlockSpec((tm, HEAD_DIM), lambda i: (prev(i) % per_b, 0))
    x2, q, k, v = pl.pallas_call(
        functools.partial(_mlp_qkv_kernel, ff_chunk=min(FF_CHUNK, ff)),
        grid=(nt + 1,),
        in_specs=[
            row(d), _resident((1, d)), _resident((d, ff)), _resident((ff, d)),
            _resident((1, d)), _resident((d, qkv_dim)), _resident((1, HEAD_DIM)), tab, tab,
        ],
        out_specs=[row(d), row(n_q), head(HEAD_DIM), head(2 * HEAD_DIM)],
        out_shape=[
            jax.ShapeDtypeStruct((b * s, d), F32),
            jax.ShapeDtypeStruct((b * s, n_q), F32),
            jax.ShapeDtypeStruct((b, N_KV_HEADS, s, HEAD_DIM), BF16),
            jax.ShapeDtypeStruct((b, N_KV_HEADS, s, 2 * HEAD_DIM), BF16),
        ],
        scratch_shapes=[pltpu.VMEM((2, tm, qkv_dim - n_q), F32)],
        compiler_params=_params(1),
        name="mlp_qkv",
    )(x.reshape(b * s, d), g.reshape(1, d), w_up.astype(BF16), w_down.astype(BF16),
      g_attn.reshape(1, d), w, gk, jnp.asarray(cos), jnp.asarray(sin))
    return x2.reshape(b, s, d), q.reshape(b, s, n_q), k, v


def _attn_kernel(q_ref, gq_ref, cos_ref, sin_ref, k_ref, v_ref, o_ref, s0, s1, p0, p1, *, tq, nq):
    s_buf, p_buf = (s0, s1), (p0, p1)
    seq = k_ref.shape[2]

    def scores(row, h, slot):
        q = _rms(q_ref[0, pl.ds(row, tq), h * HEAD_DIM:(h + 1) * HEAD_DIM], gq_ref[...])
        q = _rope(q, cos_ref[pl.ds(row, tq), :], sin_ref[pl.ds(row, tq), :]).astype(BF16)
        s_buf[slot][:, :seq] = lax.dot_general(q, k_ref[0, 0], (((1,), (1,)), ((), ())),
                                               preferred_element_type=F32)

    def softmax(slot):
        sb, pb = s_buf[slot], p_buf[slot]
        m = sb[:, 0:LANES]
        for c in range(1, seq // LANES):
            m = jnp.maximum(m, sb[:, c * LANES:(c + 1) * LANES])
        mb = jnp.broadcast_to(jnp.max(m, axis=-1, keepdims=True), (tq, LANES))
        for c in range(seq // LANES):
            p = jnp.exp2(sb[:, c * LANES:(c + 1) * LANES] - mb)
            pb[:, c * LANES:(c + 1) * LANES] = p.astype(BF16)

    def weighted_values(row, h, slot):
        o = jnp.dot(p_buf[slot][:, :seq], v_ref[0, 0], preferred_element_type=F32)
        o = o[:, :HEAD_DIM] / o[:, HEAD_DIM:HEAD_DIM + 1]
        o_ref[0, pl.ds(row, tq), h * HEAD_DIM:(h + 1) * HEAD_DIM] = o.astype(BF16)

    p1[...] = jnp.ones_like(p1)
    scores(0, 0, 0)

    def tile_ticks(qi):
        row = pl.multiple_of(qi * tq, tq)
        row_next = pl.multiple_of(jnp.minimum(qi + 1, nq - 1) * tq, tq)
        row_prev = pl.multiple_of(jnp.maximum(qi - 1, 0) * tq, tq)
        for h in range(KV_GROUP):
            slot = h % 2
            if h + 1 < KV_GROUP:
                scores(row, h + 1, 1 - slot)
            else:
                scores(row_next, 0, 1 - slot)
            softmax(slot)
            if h > 0:
                weighted_values(row, h - 1, 1 - slot)
            else:
                weighted_values(row_prev, KV_GROUP - 1, 1 - slot)

    def body(qj, carry):
        for u in range(TILES_PER_TRIP):
            tile_ticks(qj * TILES_PER_TRIP + u)
        return carry

    lax.fori_loop(0, nq // TILES_PER_TRIP, body, 0)
    weighted_values((nq - 1) * tq, KV_GROUP - 1, 1)


def _attention(q, gq, k, v):
    b, s, _ = q.shape
    tq = min(ATTN_ROWS, s)
    nq = s // tq
    assert s % tq == 0 and nq % TILES_PER_TRIP == 0
    cos, sin = _rope_tables(s)
    whole = lambda width: pl.BlockSpec((1, 1, s, width), lambda i, g: (i, g, 0, 0))
    group = pl.BlockSpec((1, s, KV_GROUP * HEAD_DIM), lambda i, g: (i, 0, g))
    return pl.pallas_call(
        functools.partial(_attn_kernel, tq=tq, nq=nq),
        grid=(b, N_KV_HEADS),
        in_specs=[group, _resident((1, HEAD_DIM)), _resident((s, HEAD_DIM)), _resident((s, HEAD_DIM)),
                  whole(HEAD_DIM), whole(2 * HEAD_DIM)],
        out_specs=group,
        out_shape=jax.ShapeDtypeStruct((b, s, N_HEADS * HEAD_DIM), BF16),
        scratch_shapes=[pltpu.VMEM((tq, s + LANES), F32), pltpu.VMEM((tq, s + LANES), F32),
                        pltpu.VMEM((tq, s + LANES), BF16), pltpu.VMEM((tq, s + LANES), BF16)],
        compiler_params=_params(2),
        name="attention",
    )(q, gq, jnp.asarray(cos), jnp.asarray(sin), k, v)


def _trunk(x, fourier_norm, fourier_w_out, attn_norm, attn_w_qkv, attn_q_norm, attn_k_norm,
           attn_w_o, mlp_norm, mlp_w_up, mlp_w_down, final_norm):
    b, s, d = x.shape
    depth = mlp_norm.shape[0]
    qkv = None
    for i in range(depth):
        j = i // 2
        g_final = final_norm if i == depth - 1 else None
        if i % 2 == 0:
            x = _fourier_layer(x, fourier_norm[j], fourier_w_out[j])
            if i + 1 < depth:
                gq, gk = _qk_gains(attn_q_norm[j], attn_k_norm[j])
                x, q, k, v = _mlp_qkv_layer(x, mlp_norm[i], mlp_w_up[i], mlp_w_down[i], attn_norm[j],
                                            attn_w_qkv[j], gk)
                qkv = (q, gq, k, v)
            else:
                x = _mlp_layer(x.reshape(b * s, d), mlp_norm[i], mlp_w_up[i], mlp_w_down[i],
                               g_final=g_final).reshape(b, s, d)
        else:
            a = _attention(*qkv)
            x = _mlp_layer(x.reshape(b * s, d), mlp_norm[i], mlp_w_up[i], mlp_w_down[i],
                           attn=a.reshape(b * s, d), w_o=attn_w_o[j],
                           g_final=g_final).reshape(b, s, d)
    return x


def kernel(x_prompt, x_sample, fourier_norm, fourier_w_out, attn_norm, attn_w_qkv, attn_q_norm,
           attn_k_norm, attn_w_o, mlp_norm, mlp_w_up, mlp_w_down, final_norm):
    weights = (fourier_norm, fourier_w_out, attn_norm, attn_w_qkv, attn_q_norm, attn_k_norm,
               attn_w_o, mlp_norm, mlp_w_up, mlp_w_down, final_norm)
    return (_trunk(x_prompt, *weights), _trunk(x_sample, *weights))
```

```python
import functools

import numpy as np
import jax
import jax.numpy as jnp
from jax import lax
from jax.experimental import pallas as pl
from jax.experimental.pallas import tpu as pltpu

F32 = jnp.float32
BF16 = jnp.bfloat16

N_FGROUPS = 8
HEAD_DIM = 128
N_HEADS = 8
N_KV_HEADS = 2
KV_GROUP = N_HEADS // N_KV_HEADS
AXIS_DIM = HEAD_DIM // 2
ROPE_THETA = 10000.0
GRID_W = 64
EPS = 1e-6

V7X_VMEM_BYTES = 64 * 1024 * 1024
VMEM_LIMIT_BYTES = V7X_VMEM_BYTES - 8 * 1024 * 1024
LANES = 128

RADIX = 8
FOURIER_ROWS = 512
MLP_ROWS = 512
FF_CHUNK = 1024
ATTN_ROWS = 256
TILES_PER_TRIP = 4


def _params(n_grid_dims):
    return pltpu.CompilerParams(
        dimension_semantics=("arbitrary",) * n_grid_dims,
        vmem_limit_bytes=VMEM_LIMIT_BYTES,
    )


def _rms(x, g):
    return x * lax.rsqrt(jnp.mean(x * x, axis=-1, keepdims=True) + EPS) * g


def _resident(shape):
    nd = len(shape)
    return pl.BlockSpec(shape, lambda *_: (0,) * nd, pipeline_mode=pl.Buffered(1))


def _fourier_tables(s, d, tk):
    n = s // RADIX
    k = np.arange(n)
    ang = 2.0 * np.pi * ((k[:, None] * k[None, :]) % n) / n
    c = np.cos(ang).reshape(n // tk, tk, n)
    sn = np.sin(ang).reshape(n // tk, tk, n)
    dft = np.concatenate([c, -sn], axis=1).astype(np.float32)
    tw = np.zeros((n, 2 * RADIX), np.float32)
    for j in range(1, RADIX):
        th = 2.0 * np.pi * ((k * j) % s) / s
        tw[:, 2 * (j - 1)] = np.cos(th)
        tw[:, 2 * (j - 1) + 1] = -np.sin(th)
    fg = d // N_FGROUPS
    cc = np.arange(fg)
    cang = 2.0 * np.pi * ((cc[:, None] * cc[None, :]) % fg) / fg
    scale = 1.0 / np.sqrt(float(s) * fg)
    chan = np.concatenate([np.cos(cang), np.sin(cang)], axis=0) * scale
    return dft, tw, chan.astype(np.float32)


def _times_root(z, m, r):
    re, im = z
    m %= r
    if m == 0:
        return re, im
    if 4 * m == r:
        return im, -re
    if 2 * m == r:
        return -re, -im
    if 4 * m == 3 * r:
        return -im, re
    wr, wi = float(np.cos(2.0 * np.pi * m / r)), float(-np.sin(2.0 * np.pi * m / r))
    if 8 * m % r == 0:
        a = abs(wr)
        sr, si = np.sign(wr), np.sign(wi)
        return (sr * re - si * im) * a, (si * re + sr * im) * a
    return re * wr - im * wi, re * wi + im * wr


def _small_dft(zs):
    r = len(zs)
    if r == 1:
        return zs
    even, odd = _small_dft(zs[0::2]), _small_dft(zs[1::2])
    out = [None] * r
    for q in range(r // 2):
        tr, ti = _times_root(odd[q], q, r)
        out[q] = (even[q][0] + tr, even[q][1] + ti)
        out[q + r // 2] = (even[q][0] - tr, even[q][1] - ti)
    return out


def _normalise_rows(xt_ref, g_ref, cols_ref, xr_ref, j, d):
    rows = xt_ref.shape[1] // RADIX
    xn = _rms(xt_ref[0], g_ref[...])
    for c in range(d // LANES):
        cols_ref[c] = xn[:, c * LANES:(c + 1) * LANES]
    r0 = pl.multiple_of(j * rows, rows)
    for jr in range(RADIX):
        for c in range(d // LANES):
            xr_ref[pl.ds(r0, rows), jr * d + c * LANES:jr * d + (c + 1) * LANES] = (
                cols_ref[c, pl.ds(jr, rows, stride=RADIX), :].astype(BF16))


def _fourier_tile(dft_ref, xr_ref, tw_ref, chan_ref, w_ref, o_ref, tk, d):
    e = jnp.dot(dft_ref[0], xr_ref[...], preferred_element_type=F32)
    tw = tw_ref[...]
    zs = [(e[:tk, 0:d], e[tk:, 0:d])]
    for j in range(1, RADIX):
        er, ei = e[:tk, j * d:(j + 1) * d], e[tk:, j * d:(j + 1) * d]
        tr = tw[:, 2 * (j - 1):2 * (j - 1) + 1]
        ti = tw[:, 2 * (j - 1) + 1:2 * (j - 1) + 2]
        zs.append((er * tr - ei * ti, er * ti + ei * tr))
    us = _small_dft(zs)
    ur = jnp.concatenate([u[0] for u in us], axis=0).astype(BF16)
    ui = jnp.concatenate([u[1] for u in us], axis=0).astype(BF16)
    fg = d // N_FGROUPS
    chan = chan_ref[...]
    f = []
    for g in range(N_FGROUPS):
        lhs = jnp.concatenate([ur[:, g * fg:(g + 1) * fg], ui[:, g * fg:(g + 1) * fg]], axis=1)
        f.append(jnp.dot(lhs, chan, preferred_element_type=F32))
    f = jnp.concatenate(f, axis=1).astype(BF16)
    y = jnp.dot(f, w_ref[...], preferred_element_type=F32)
    o_ref[0] = y.reshape(RADIX, tk, d)


def _fourier_kernel(dft_ref, xt_ref, g_ref, tw_ref, chan_ref, w_ref, o_ref,
                    xr0, xr1, cols_ref, *, tk, d):
    i, j = pl.program_id(0), pl.program_id(1)

    def step(xr_fill, xr_mix):
        _fourier_tile(dft_ref, xr_mix, tw_ref, chan_ref, w_ref, o_ref, tk, d)
        _normalise_rows(xt_ref, g_ref, cols_ref, xr_fill, j, d)

    @pl.when(i == 0)
    def _():
        _normalise_rows(xt_ref, g_ref, cols_ref, xr0, j, d)

    @pl.when((i > 0) & (i % 2 == 0))
    def _():
        step(xr0, xr1)

    @pl.when(i % 2 == 1)
    def _():
        step(xr1, xr0)


def _fourier_layer(x, g, w_out):
    b, s, d = x.shape
    n = s // RADIX
    tk = min(FOURIER_ROWS // RADIX, n)
    nk = n // tk
    tokens = s // nk
    dft, tw, chan = _fourier_tables(s, d, tk)
    dft = jnp.asarray(dft).astype(BF16)
    chan = jnp.asarray(chan).astype(BF16)
    fg = d // N_FGROUPS
    fill = lambda i: jnp.minimum(i, b - 1)
    mix = lambda i: jnp.maximum(i - 1, 0)
    blocks = pl.BlockSpec((1, RADIX, tk, d), lambda i, j: (mix(i), 0, jnp.where(i == 0, 0, j), 0))
    out = pl.pallas_call(
        functools.partial(_fourier_kernel, tk=tk, d=d),
        grid=(b + 1, nk),
        in_specs=[
            pl.BlockSpec((1, 2 * tk, n), lambda i, j: (j, 0, 0)),
            pl.BlockSpec((1, tokens, d), lambda i, j: (fill(i), j, 0)),
            _resident((1, d)),
            pl.BlockSpec((tk, 2 * RADIX), lambda i, j: (j, 0)),
            _resident((2 * fg, fg)),
            _resident((d, d)),
        ],
        out_specs=blocks,
        out_shape=jax.ShapeDtypeStruct((b, RADIX, n, d), F32),
        scratch_shapes=[pltpu.VMEM((n, RADIX * d), BF16), pltpu.VMEM((n, RADIX * d), BF16),
                        pltpu.VMEM((d // LANES, tokens, LANES), F32)],
        compiler_params=_params(2),
        name="fourier",
    )(dft, x, g.reshape(1, d), jnp.asarray(tw), chan, w_out.astype(BF16))
    return out.reshape(b, s, d)


def _head_lane_order():
    quarter = AXIS_DIM // 2
    blocks = np.arange(HEAD_DIM).reshape(4, quarter)
    return np.concatenate([blocks[0], blocks[2], blocks[1], blocks[3]])


def _rope_tables(s):
    pos = np.arange(s)
    inv = ROPE_THETA ** (-np.arange(0, AXIS_DIM, 2, dtype=np.float64) / AXIS_DIM)
    ang = np.concatenate([(pos // GRID_W)[:, None] * inv[None, :],
                          (pos % GRID_W)[:, None] * inv[None, :]], axis=1)
    cos = np.concatenate([np.cos(ang), np.cos(ang)], axis=1)
    sin = np.concatenate([-np.sin(ang), np.sin(ang)], axis=1)
    return cos.astype(np.float32), sin.astype(np.float32)


def _rope(x, cos, sin):
    return x * cos + pltpu.roll(x, HEAD_DIM // 2, 1) * sin


def _store_kv_heads(raw, gk_ref, cos_ref, sin_ref, k_ref, v_ref):
    cos, sin = cos_ref[...], sin_ref[...]
    lane = lax.broadcasted_iota(jnp.int32, cos.shape, 1)
    ones_col = jnp.where(lane == 0, 1.0, 0.0).astype(BF16)
    for h in range(N_KV_HEADS):
        kh = _rms(raw[:, h * HEAD_DIM:(h + 1) * HEAD_DIM], gk_ref[...])
        k_ref[0, h] = _rope(kh, cos, sin).astype(BF16)
        off = (N_KV_HEADS + h) * HEAD_DIM
        v_ref[0, h, :, :HEAD_DIM] = raw[:, off:off + HEAD_DIM].astype(BF16)
        v_ref[0, h, :, HEAD_DIM:] = ones_col


def _mlp_block(x, g_ref, wup_ref, wdown_ref, ff_chunk):
    xn = _rms(x, g_ref[...]).astype(BF16)
    acc = x
    for c in range(wup_ref.shape[1] // ff_chunk):
        h = jnp.dot(xn, wup_ref[:, c * ff_chunk:(c + 1) * ff_chunk], preferred_element_type=F32)
        h = jnp.maximum(h, 0.0)
        h = (h * h).astype(BF16)
        acc = acc + jnp.dot(h, wdown_ref[c * ff_chunk:(c + 1) * ff_chunk, :],
                            preferred_element_type=F32)
    return acc


def _mlp_kernel(*refs, has_mix, has_proj, has_final, ff_chunk):
    refs = list(refs)
    x_ref = refs.pop(0)
    if has_mix:
        y_ref = refs.pop(0)
    if has_proj:
        a_ref, wo_ref = refs.pop(0), refs.pop(0)
    g_ref, wup_ref, wdown_ref = refs.pop(0), refs.pop(0), refs.pop(0)
    if has_final:
        gf_ref = refs.pop(0)
    o_ref = refs.pop(0)

    x = x_ref[...]
    if has_mix:
        x = x + y_ref[...]
    if has_proj:
        x = x + jnp.dot(a_ref[...], wo_ref[...], preferred_element_type=F32)
    acc = _mlp_block(x, g_ref, wup_ref, wdown_ref, ff_chunk)
    if has_final:
        acc = _rms(acc, gf_ref[...])
    o_ref[...] = acc


def _mlp_layer(x2d, g, w_up, w_down, mix=None, attn=None, w_o=None, g_final=None):
    t, d = x2d.shape
    ff = w_up.shape[1]
    tm = min(MLP_ROWS, t)
    has_proj = attn is not None
    has_final = g_final is not None
    row = pl.BlockSpec((tm, d), lambda i: (i, 0))
    args, specs = [x2d], [row]
    has_mix = mix is not None
    if has_mix:
        args.append(mix)
        specs.append(row)
    if has_proj:
        args += [attn, w_o.astype(BF16)]
        specs += [row, _resident((d, d))]
    args += [g.reshape(1, d), w_up.astype(BF16), w_down.astype(BF16)]
    specs += [_resident((1, d)), _resident((d, ff)), _resident((ff, d))]
    if has_final:
        args.append(g_final.reshape(1, d))
        specs.append(_resident((1, d)))
    return pl.pallas_call(
        functools.partial(_mlp_kernel, has_mix=has_mix, has_proj=has_proj, has_final=has_final,
                          ff_chunk=min(FF_CHUNK, ff)),
        grid=(t // tm,),
        in_specs=specs,
        out_specs=row,
        out_shape=jax.ShapeDtypeStruct((t, d), F32),
        compiler_params=_params(1),
        name="mlp_proj" if has_proj else "mlp",
    )(*args)


def _mlp_qkv_kernel(x_ref, y_ref, g_ref, wup_ref, wdown_ref, ga_ref, wqkv_ref, gk_ref,
                    cos_ref, sin_ref, o_ref, q_ref, k_ref, v_ref, raw_ref, *, ff_chunk):
    i = pl.program_id(0)
    slot = i % 2
    n_q = q_ref.shape[1]

    @pl.when(i == 0)
    def _():
        raw_ref[1] = jnp.zeros(raw_ref.shape[1:], F32)

    _store_kv_heads(raw_ref.at[1 - slot], gk_ref, cos_ref, sin_ref, k_ref, v_ref)
    acc = _mlp_block(x_ref[...] + y_ref[...], g_ref, wup_ref, wdown_ref, ff_chunk)
    o_ref[...] = acc
    xn = _rms(acc, ga_ref[...]).astype(BF16)
    qkv = jnp.dot(xn, wqkv_ref[...], preferred_element_type=F32)
    q_ref[...] = qkv[:, :n_q]
    raw_ref[slot] = qkv[:, n_q:]


def _qk_gains(gq, gk):
    order = _head_lane_order()
    q_scale = HEAD_DIM ** -0.5 * np.log2(np.e)
    return (gq[order] * q_scale).reshape(1, HEAD_DIM), gk[order].reshape(1, HEAD_DIM)


def _mlp_qkv_layer(x, mix, g, w_up, w_down, g_attn, w_qkv, gk):
    b, s, d = x.shape
    ff = w_up.shape[1]
    tm = min(MLP_ROWS, s)
    per_b = s // tm
    nt = b * per_b
    cos, sin = _rope_tables(s)
    qkv_dim = w_qkv.shape[1]
    n_q = N_HEADS * HEAD_DIM
    n_qk = (N_HEADS + N_KV_HEADS) * HEAD_DIM
    heads = np.arange(N_HEADS + N_KV_HEADS)[:, None] * HEAD_DIM
    cols = np.concatenate([(heads + _head_lane_order()[None, :]).ravel(), np.arange(n_qk, qkv_dim)])
    w = w_qkv[:, cols].astype(BF16)
    cur = lambda i: jnp.minimum(i, nt - 1)
    prev = lambda i: jnp.maximum(i - 1, 0)
    row = lambda width: pl.BlockSpec((tm, width), lambda i: (cur(i), 0))
    head = lambda width: pl.BlockSpec(
        (1, N_KV_HEADS, tm, width), lambda i: (prev(i) // per_b, 0, prev(i) % per_b, 0))
    tab = pl.BlockSpec((tm, HEAD_DIM), lambda i: (prev(i) % per_b, 0))
    x2, q, k, v = pl.pallas_call(
        functools.partial(_mlp_qkv_kernel, ff_chunk=min(FF_CHUNK, ff)),
        grid=(nt + 1,),
        in_specs=[
            row(d), row(d), _resident((1, d)), _resident((d, ff)), _resident((ff, d)),
            _resident((1, d)), _resident((d, qkv_dim)), _resident((1, HEAD_DIM)), tab, tab,
        ],
        out_specs=[row(d), row(n_q), head(HEAD_DIM), head(2 * HEAD_DIM)],
        out_shape=[
            jax.ShapeDtypeStruct((b * s, d), F32),
            jax.ShapeDtypeStruct((b * s, n_q), F32),
            jax.ShapeDtypeStruct((b, N_KV_HEADS, s, HEAD_DIM), BF16),
            jax.ShapeDtypeStruct((b, N_KV_HEADS, s, 2 * HEAD_DIM), BF16),
        ],
        scratch_shapes=[pltpu.VMEM((2, tm, qkv_dim - n_q), F32)],
        compiler_params=_params(1),
        name="mlp_qkv",
    )(x.reshape(b * s, d), mix.reshape(b * s, d), g.reshape(1, d), w_up.astype(BF16),
      w_down.astype(BF16), g_attn.reshape(1, d), w, gk, jnp.asarray(cos), jnp.asarray(sin))
    return x2.reshape(b, s, d), q.reshape(b, s, n_q), k, v


def _attn_kernel(q_ref, gq_ref, cos_ref, sin_ref, k_ref, v_ref, o_ref, s0, s1, p0, p1, *, tq, nq):
    s_buf, p_buf = (s0, s1), (p0, p1)
    seq = k_ref.shape[2]

    def scores(row, h, slot):
        q = _rms(q_ref[0, pl.ds(row, tq), h * HEAD_DIM:(h + 1) * HEAD_DIM], gq_ref[...])
        q = _rope(q, cos_ref[pl.ds(row, tq), :], sin_ref[pl.ds(row, tq), :]).astype(BF16)
        s_buf[slot][:, :seq] = lax.dot_general(q, k_ref[0, 0], (((1,), (1,)), ((), ())),
                                               preferred_element_type=F32)

    def softmax(slot):
        sb, pb = s_buf[slot], p_buf[slot]
        m = sb[:, 0:LANES]
        for c in range(1, seq // LANES):
            m = jnp.maximum(m, sb[:, c * LANES:(c + 1) * LANES])
        mb = jnp.broadcast_to(jnp.max(m, axis=-1, keepdims=True), (tq, LANES))
        for c in range(seq // LANES):
            p = jnp.exp2(sb[:, c * LANES:(c + 1) * LANES] - mb)
            pb[:, c * LANES:(c + 1) * LANES] = p.astype(BF16)

    def weighted_values(row, h, slot):
        o = jnp.dot(p_buf[slot][:, :seq], v_ref[0, 0], preferred_element_type=F32)
        o = o[:, :HEAD_DIM] / o[:, HEAD_DIM:HEAD_DIM + 1]
        o_ref[0, pl.ds(row, tq), h * HEAD_DIM:(h + 1) * HEAD_DIM] = o.astype(BF16)

    p1[...] = jnp.ones_like(p1)
    scores(0, 0, 0)

    def tile_ticks(qi):
        row = pl.multiple_of(qi * tq, tq)
        row_next = pl.multiple_of(jnp.minimum(qi + 1, nq - 1) * tq, tq)
        row_prev = pl.multiple_of(jnp.maximum(qi - 1, 0) * tq, tq)
        for h in range(KV_GROUP):
            slot = h % 2
            if h + 1 < KV_GROUP:
                scores(row, h + 1, 1 - slot)
            else:
                scores(row_next, 0, 1 - slot)
            softmax(slot)
            if h > 0:
                weighted_values(row, h - 1, 1 - slot)
            else:
                weighted_values(row_prev, KV_GROUP - 1, 1 - slot)

    def body(qj, carry):
        for u in range(TILES_PER_TRIP):
            tile_ticks(qj * TILES_PER_TRIP + u)
        return carry

    lax.fori_loop(0, nq // TILES_PER_TRIP, body, 0)
    weighted_values((nq - 1) * tq, KV_GROUP - 1, 1)


def _attention(q, gq, k, v):
    b, s, _ = q.shape
    tq = min(ATTN_ROWS, s)
    nq = s // tq
    assert s % tq == 0 and nq % TILES_PER_TRIP == 0
    cos, sin = _rope_tables(s)
    whole = lambda width: pl.BlockSpec((1, 1, s, width), lambda i, g: (i, g, 0, 0))
    group = pl.BlockSpec((1, s, KV_GROUP * HEAD_DIM), lambda i, g: (i, 0, g))
    return pl.pallas_call(
        functools.partial(_attn_kernel, tq=tq, nq=nq),
        grid=(b, N_KV_HEADS),
        in_specs=[group, _resident((1, HEAD_DIM)), _resident((s, HEAD_DIM)), _resident((s, HEAD_DIM)),
                  whole(HEAD_DIM), whole(2 * HEAD_DIM)],
        out_specs=group,
        out_shape=jax.ShapeDtypeStruct((b, s, N_HEADS * HEAD_DIM), BF16),
        scratch_shapes=[pltpu.VMEM((tq, s + LANES), F32), pltpu.VMEM((tq, s + LANES), F32),
                        pltpu.VMEM((tq, s + LANES), BF16), pltpu.VMEM((tq, s + LANES), BF16)],
        compiler_params=_params(2),
        name="attention",
    )(q, gq, jnp.asarray(cos), jnp.asarray(sin), k, v)


def _trunk(x, fourier_norm, fourier_w_out, attn_norm, attn_w_qkv, attn_q_norm, attn_k_norm,
           attn_w_o, mlp_norm, mlp_w_up, mlp_w_down, final_norm):
    b, s, d = x.shape
    depth = mlp_norm.shape[0]
    qkv = None
    for i in range(depth):
        j = i // 2
        g_final = final_norm if i == depth - 1 else None
        if i % 2 == 0:
            mix = _fourier_layer(x, fourier_norm[j], fourier_w_out[j])
            if i + 1 < depth:
                gq, gk = _qk_gains(attn_q_norm[j], attn_k_norm[j])
                x, q, k, v = _mlp_qkv_layer(x, mix, mlp_norm[i], mlp_w_up[i], mlp_w_down[i], attn_norm[j],
                                            attn_w_qkv[j], gk)
                qkv = (q, gq, k, v)
            else:
                x = _mlp_layer(x.reshape(b * s, d), mlp_norm[i], mlp_w_up[i], mlp_w_down[i],
                               mix=mix.reshape(b * s, d), g_final=g_final).reshape(b, s, d)
        else:
            a = _attention(*qkv)
            x = _mlp_layer(x.reshape(b * s, d), mlp_norm[i], mlp_w_up[i], mlp_w_down[i],
                           attn=a.reshape(b * s, d), w_o=attn_w_o[j],
                           g_final=g_final).reshape(b, s, d)
    return x


def kernel(x_prompt, x_sample, fourier_norm, fourier_w_out, attn_norm, attn_w_qkv, attn_q_norm,
           attn_k_norm, attn_w_o, mlp_norm, mlp_w_up, mlp_w_down, final_norm):
    weights = (fourier_norm, fourier_w_out, attn_norm, attn_w_qkv, attn_q_norm, attn_k_norm,
               attn_w_o, mlp_norm, mlp_w_up, mlp_w_down, final_norm)
    return (_trunk(x_prompt, *weights), _trunk(x_sample, *weights))
```
